```python
import math
import jax
import jax.numpy as jnp
from jax import lax
import numpy as np

D_MODEL = 4096
BATCH = 16
SEQ = 256
DEPTH = 4
DEC_BATCH = 8
DEC_SEQ = 4096
PAST_LEN = 256

GRID_W = 64
HEAD_DIM = 128
N_HEADS = (D_MODEL // 2) // HEAD_DIM
N_KV_HEADS = N_HEADS // 4
GROUP = N_HEADS // N_KV_HEADS
Q_W = N_HEADS * HEAD_DIM
KV_W = N_KV_HEADS * HEAD_DIM
F_W = D_MODEL // 4
F_GROUPS = 8
F_GC = F_W // F_GROUPS
CONV_CH = D_MODEL // 4
CONV_K = 31
IN_COLS = F_W + 2 * CONV_CH + Q_W + 2 * KV_W
MIX_W = F_W + CONV_CH + Q_W
SPLITS = (F_W, F_W + 2 * CONV_CH, F_W + 2 * CONV_CH + Q_W, F_W + 2 * CONV_CH + Q_W + KV_W)
D_FF = 4 * D_MODEL
WINDOW = 128
BLK = 128
ROT_AXIS = HEAD_DIM // 2
ROPE_BASE = 10000.0
SCALE = HEAD_DIM ** -0.5
EPS = 1e-6
NEG_INF = -1e30

kernel_name = "hybrid_fourier_conformer_swa_dit_step"


def rmsnorm(x, g):
    xf = x.astype(jnp.float32)
    y = xf * lax.rsqrt(jnp.mean(xf * xf, axis=-1, keepdims=True) + EPS)
    return (y * g.astype(jnp.float32)).astype(x.dtype)


def layernorm(x, g, b):
    xf = x.astype(jnp.float32)
    mu = jnp.mean(xf, axis=-1, keepdims=True)
    var = jnp.mean(jnp.square(xf - mu), axis=-1, keepdims=True)
    y = (xf - mu) * lax.rsqrt(var + EPS) * g.astype(jnp.float32) + b.astype(jnp.float32)
    return y.astype(x.dtype)


def axial_angles(T):
    rows = T // GRID_W
    row = jnp.broadcast_to(jnp.arange(rows)[:, None], (rows, GRID_W)).reshape(T).astype(jnp.float32)
    col = jnp.broadcast_to(jnp.arange(GRID_W)[None, :], (rows, GRID_W)).reshape(T).astype(jnp.float32)
    inv = ROPE_BASE ** (-jnp.arange(0, ROT_AXIS, 2, dtype=jnp.float32) / ROT_AXIS)
    return row[:, None] * inv, col[:, None] * inv


def rope_axis(x, ang):
    x1, x2 = jnp.split(x, 2, axis=-1)
    cos = jnp.cos(ang)[None, :, None, :].astype(x.dtype)
    sin = jnp.sin(ang)[None, :, None, :].astype(x.dtype)
    return jnp.concatenate([x1 * cos - x2 * sin, x2 * cos + x1 * sin], axis=-1)


def apply_axial_rope(x, ang_row, ang_col):
    xr, xc = jnp.split(x, 2, axis=-1)
    return jnp.concatenate([rope_axis(xr, ang_row), rope_axis(xc, ang_col)], axis=-1)


def fourier_mix(uf, w_f):
    B, T, _ = uf.shape
    xg = uf.reshape(B, T, F_GROUPS, F_GC).astype(jnp.float32)
    xr = jnp.fft.fft2(xg, axes=(1, 3), norm="ortho").real.astype(uf.dtype)
    y = jnp.einsum("btgc,gcd->btgd", xr, w_f)
    return y.reshape(B, T, F_W)


def conformer_conv(ug, w_dw, b_dw, cln_g, cln_b, w_pw):
    a, gte = jnp.split(ug, 2, axis=-1)
    h = a * jax.nn.sigmoid(gte)
    h = lax.conv_general_dilated(
        h, w_dw[:, None, :].astype(h.dtype), window_strides=(1,),
        padding=[((CONV_K - 1) // 2, (CONV_K - 1) // 2)],
        dimension_numbers=("NWC", "WIO", "NWC"), feature_group_count=CONV_CH) + b_dw
    h = layernorm(h, cln_g, cln_b)
    h = jax.nn.silu(h)
    return h @ w_pw


def sink_softmax_av(s, sink, v):
    sk = jnp.broadcast_to(sink.reshape(N_KV_HEADS, GROUP)[None, :, :, None, None].astype(jnp.float32),
                          s.shape[:-1] + (1,))
    p = jax.nn.softmax(jnp.concatenate([sk, s], axis=-1), axis=-1)[..., 1:]
    return jnp.einsum("bkgqj,bjkd->bqkgd", p.astype(v.dtype), v)


def context_attention(q, k, v, sink):
    B, T, H, D = q.shape
    nq = T // BLK
    qb = jnp.moveaxis(q.reshape(B, nq, BLK, N_KV_HEADS, GROUP, D), 1, 0)

    def one(qi):
        s = jnp.einsum("bqkgd,bjkd->bkgqj", qi, k).astype(jnp.float32) * SCALE
        return sink_softmax_av(s, sink, v)

    out = lax.map(one, qb)
    return jnp.moveaxis(out, 0, 1).reshape(B, T, H, D)


def latent_attention(q, k, v, ck, cv, sink):
    B, T, H, D = q.shape
    nb = T // BLK
    qb = jnp.moveaxis(q.reshape(B, nb, BLK, N_KV_HEADS, GROUP, D), 1, 0)
    pad = ((0, 0), (BLK, BLK), (0, 0), (0, 0))
    kb = jnp.pad(k, pad).reshape(B, nb + 2, BLK, N_KV_HEADS, D)
    vb = jnp.pad(v, pad).reshape(B, nb + 2, BLK, N_KV_HEADS, D)
    kw = jnp.moveaxis(jnp.concatenate([kb[:, :-2], kb[:, 1:-1], kb[:, 2:]], axis=2), 1, 0)
    vw = jnp.moveaxis(jnp.concatenate([vb[:, :-2], vb[:, 1:-1], vb[:, 2:]], axis=2), 1, 0)
    bi = jnp.arange(nb)[:, None, None]
    qpos = bi * BLK + jnp.arange(BLK)[None, :, None]
    kpos = bi * BLK - BLK + jnp.arange(3 * BLK)[None, None, :]
    win_mask = (jnp.abs(kpos - qpos) <= WINDOW) & (kpos >= 0) & (kpos < T)
    P = ck.shape[1]
    ctx_mask = jnp.ones((BLK, P), dtype=bool)

    def one(args):
        qi, kwi, vwi, mi = args
        kk = jnp.concatenate([ck, kwi], axis=1)
        vv = jnp.concatenate([cv, vwi], axis=1)
        s = jnp.einsum("bqkgd,bjkd->bkgqj", qi, kk).astype(jnp.float32) * SCALE
        valid = jnp.concatenate([ctx_mask, mi], axis=-1)
        s = jnp.where(valid[None, None, None], s, NEG_INF)
        return sink_softmax_av(s, sink, vv)

    out = lax.map(one, (qb, kw, vw, win_mask))
    return jnp.moveaxis(out, 0, 1).reshape(B, T, H, D)


def trunk_layer(x, mod, g1, g2, w_in, w_f, w_dw, b_dw, cln_g, cln_b, w_pw, sink, w_out,
                w1, b1, w2, b2, ctx_kv):
    shift_a, scale_a, gate_a, shift_m, scale_m, gate_m = jnp.split(mod, 6, axis=-1)
    B, T, _ = x.shape
    n = rmsnorm(x, g1) * (1 + scale_a) + shift_a
    u = n @ w_in
    uf, ug, q, k, v = jnp.split(u, SPLITS, axis=-1)
    q = q.reshape(B, T, N_HEADS, HEAD_DIM)
    k = k.reshape(B, T, N_KV_HEADS, HEAD_DIM)
    v = v.reshape(B, T, N_KV_HEADS, HEAD_DIM)
    if ctx_kv is None:
        att = context_attention(q, k, v, sink)
    else:
        ang_r, ang_c = axial_angles(T)
        q = apply_axial_rope(q, ang_r, ang_c)
        k = apply_axial_rope(k, ang_r, ang_c)
        att = latent_attention(q, k, v, ctx_kv[0], ctx_kv[1], sink)
    mix = jnp.concatenate([
        fourier_mix(uf, w_f),
        conformer_conv(ug, w_dw, b_dw, cln_g, cln_b, w_pw),
        att.reshape(B, T, Q_W)], axis=-1)
    x = x + gate_a * (mix @ w_out)
    n2 = rmsnorm(x, g2) * (1 + scale_m) + shift_m
    h = jnp.square(jax.nn.relu(n2 @ w1 + b1))
    x = x + gate_m * (h @ w2 + b2)
    return x, k, v


def setup_inputs(seed: int = 0) -> dict:
    key = jax.random.key(seed)
    ks = jax.random.split(key, 32)
    f32 = jnp.float32

    def nrm(k, shape, scale):
        return jax.random.normal(k, shape, f32) * scale

    return {
        "x_prompt": nrm(ks[0], (BATCH, SEQ, D_MODEL), 1.0),
        "x_sample": nrm(ks[1], (DEC_BATCH, DEC_SEQ, D_MODEL), 1.0),
        "cache_k": nrm(ks[2], (DEC_BATCH, DEPTH, PAST_LEN, N_KV_HEADS, HEAD_DIM), 1.0),
        "cache_v": nrm(ks[3], (DEC_BATCH, DEPTH, PAST_LEN, N_KV_HEADS, HEAD_DIM), 1.0),
        "c": nrm(ks[4], (DEC_BATCH, D_MODEL), 1.0),
        "c_ctx": nrm(ks[5], (D_MODEL,), 1.0),
        "norm1_g": 1.0 + nrm(ks[6], (DEPTH, D_MODEL), 0.1),
        "norm2_g": 1.0 + nrm(ks[7], (DEPTH, D_MODEL), 0.1),
        "w_ada": nrm(ks[8], (DEPTH, D_MODEL, 6 * D_MODEL), D_MODEL ** -0.5),
        "b_ada": nrm(ks[9], (DEPTH, 6 * D_MODEL), 0.02),
        "w_in": nrm(ks[10], (DEPTH, D_MODEL, IN_COLS), D_MODEL ** -0.5),
        "w_fourier": nrm(ks[11], (DEPTH, F_GROUPS, F_GC, F_GC), F_GC ** -0.5),
        "w_dw": nrm(ks[12], (DEPTH, CONV_K, CONV_CH), CONV_K ** -0.5),
        "b_dw": nrm(ks[13], (DEPTH, CONV_CH), 0.02),
        "conv_ln_g": 1.0 + nrm(ks[14], (DEPTH, CONV_CH), 0.1),
        "conv_ln_b": nrm(ks[15], (DEPTH, CONV_CH), 0.02),
        "w_pw": nrm(ks[16], (DEPTH, CONV_CH, CONV_CH), CONV_CH ** -0.5),
        "sink": nrm(ks[17], (DEPTH, N_HEADS), 0.5),
        "w_out": nrm(ks[18], (DEPTH, MIX_W, D_MODEL), MIX_W ** -0.5),
        "w_mlp1": nrm(ks[19], (DEPTH, D_MODEL, D_FF), D_MODEL ** -0.5),
        "b_mlp1": nrm(ks[20], (DEPTH, D_FF), 0.02),
        "w_mlp2": nrm(ks[21], (DEPTH, D_FF, D_MODEL), D_FF ** -0.5),
        "b_mlp2": nrm(ks[22], (DEPTH, D_MODEL), 0.02),
        "final_g": 1.0 + nrm(ks[23], (D_MODEL,), 0.1),
    }


def reference(x_prompt, x_sample, cache_k, cache_v, c, c_ctx, norm1_g, norm2_g, w_ada, b_ada,
              w_in, w_fourier, w_dw, b_dw, conv_ln_g, conv_ln_b, w_pw, sink, w_out,
              w_mlp1, b_mlp1, w_mlp2, b_mlp2, final_g):
    s_ctx = jax.nn.silu(c_ctx)
    s_lat = jax.nn.silu(c)

    xp = x_prompt
    ks_out = []
    vs_out = []
    for l in range(DEPTH):
        mod_ctx = (s_ctx @ w_ada[l] + b_ada[l])[None, None, :]
        xp, k_l, v_l = trunk_layer(
            xp, mod_ctx, norm1_g[l], norm2_g[l], w_in[l], w_fourier[l], w_dw[l], b_dw[l],
            conv_ln_g[l], conv_ln_b[l], w_pw[l], sink[l], w_out[l],
            w_mlp1[l], b_mlp1[l], w_mlp2[l], b_mlp2[l], None)
        ks_out.append(k_l)
        vs_out.append(v_l)
    y_prompt = rmsnorm(xp, final_g)
    new_cache_k = jnp.stack(ks_out, axis=1)
    new_cache_v = jnp.stack(vs_out, axis=1)

    xs = x_sample
    for l in range(DEPTH):
        mod_lat = (s_lat @ w_ada[l] + b_ada[l])[:, None, :]
        xs, _, _ = trunk_layer(
            xs, mod_lat, norm1_g[l], norm2_g[l], w_in[l], w_fourier[l], w_dw[l], b_dw[l],
            conv_ln_g[l], conv_ln_b[l], w_pw[l], sink[l], w_out[l],
            w_mlp1[l], b_mlp1[l], w_mlp2[l], b_mlp2[l], (cache_k[:, l], cache_v[:, l]))
    y_sample = rmsnorm(xs, final_g)

    return (y_prompt, y_sample, new_cache_k, new_cache_v)
```

```python
import functools
import math

import jax
import jax.numpy as jnp
from jax import lax
from jax.experimental import pallas as pl
from jax.experimental.pallas import tpu as pltpu

GRID_W = 64
HEAD_DIM = 128
GROUP = 4
F_GROUPS = 8
CONV_K = 31
CONV_HALO = 16
SUBLANES = 8
WINDOW = 128
BLK = 128
ROPE_BASE = 10000.0
EPS = 1e-6
NEG_INF = -1e30
SCALE = HEAD_DIM ** -0.5

V7X_VMEM_BYTES = 64 * 1024 * 1024
VMEM_LIMIT_BYTES = V7X_VMEM_BYTES - 8 * 1024 * 1024

F32 = jnp.float32
BF16 = jnp.bfloat16


def _params(*semantics):
    return pltpu.CompilerParams(dimension_semantics=semantics, vmem_limit_bytes=VMEM_LIMIT_BYTES)


def _pick(n, candidates):
    for c in candidates:
        if n % c == 0:
            return c
    raise ValueError(f"no tile in {candidates} divides {n}")


class _Rows:
    def __init__(self, n_ctx, t_lat):
        self.n_ctx, self.t_lat = n_ctx, t_lat

    def mod_row(self, i, tm):
        n_ctx_tiles = self.n_ctx // tm
        per_seq = self.t_lat // tm
        return jnp.where(i < n_ctx_tiles, 0, 1 + jnp.maximum(i - n_ctx_tiles, 0) // per_seq)


def _split_bf16(v):
    hi = v.astype(BF16)
    return hi, (v - hi.astype(F32)).astype(BF16)


def _ada_kernel(c_ref, w_ref, b_ref, o_ref, *, k_chunk):
    acc = b_ref[...] + jnp.zeros(o_ref.shape, F32)
    for k0 in range(0, c_ref.shape[1], k_chunk):
        c = c_ref[:, k0:k0 + k_chunk]
        s_hi, s_lo = _split_bf16(c * jax.nn.sigmoid(c))
        w_hi, w_lo = _split_bf16(w_ref[k0:k0 + k_chunk, :])
        acc = acc + jnp.dot(s_hi, w_hi, preferred_element_type=F32)
        acc = acc + jnp.dot(s_lo, w_hi, preferred_element_type=F32)
        acc = acc + jnp.dot(s_hi, w_lo, preferred_element_type=F32)
    o_ref[...] = acc


def _ada(cond, w_ada, b_ada):
    depth, d, n = w_ada.shape
    rows = cond.shape[0]
    tn = _pick(n, (512, 256, 128))
    return pl.pallas_call(
        functools.partial(_ada_kernel, k_chunk=_pick(d, (512, 256, 128))),
        grid=(depth, n // tn),
        in_specs=[
            pl.BlockSpec((rows, d), lambda l, j: (0, 0)),
            pl.BlockSpec((None, d, tn), lambda l, j: (l, 0, j)),
            pl.BlockSpec((None, 1, tn), lambda l, j: (l, 0, j)),
        ],
        out_specs=pl.BlockSpec((None, rows, tn), lambda l, j: (l, 0, j)),
        out_shape=jax.ShapeDtypeStruct((depth, rows, n), F32),
        compiler_params=_params("parallel", "parallel"),
        name="ada_modulation",
    )(cond, w_ada, b_ada.reshape(depth, 1, n))


def _norm_mod_kernel(x_ref, g_ref, scale_ref, shift_ref, o_ref):
    x = x_ref[...]
    y = x * lax.rsqrt(jnp.mean(x * x, axis=-1, keepdims=True) + EPS) * g_ref[...]
    o_ref[...] = (y * (1.0 + scale_ref[...]) + shift_ref[...]).astype(o_ref.dtype)


def _norm_mod(x, g, mod, layer, which_scale, which_shift, rows):
    m, d = x.shape
    tm = _pick(math.gcd(rows.n_ctx, rows.t_lat), (256, 128))
    mod_spec = lambda which: pl.BlockSpec(
        (None, None, None, 1, d), lambda i: (layer, rows.mod_row(i, tm), which, 0, 0))
    return pl.pallas_call(
        _norm_mod_kernel,
        grid=(m // tm,),
        in_specs=[
            pl.BlockSpec((tm, d), lambda i: (i, 0)),
            pl.BlockSpec((None, 1, d), lambda i: (layer, 0, 0)),
            mod_spec(which_scale),
            mod_spec(which_shift),
        ],
        out_specs=pl.BlockSpec((tm, d), lambda i: (i, 0)),
        out_shape=jax.ShapeDtypeStruct((m, d), BF16),
        compiler_params=_params("parallel"),
        name="rmsnorm_modulate",
    )(x, g.reshape(g.shape[0], 1, d), mod, mod)


def _final_norm_kernel(x_ref, g_ref, o_ref):
    x = x_ref[...]
    y = x * lax.rsqrt(jnp.mean(x * x, axis=-1, keepdims=True) + EPS)
    o_ref[...] = y * g_ref[...]


def _final_norm(x, g, row0, n_rows):
    d = x.shape[1]
    tm = _pick(math.gcd(row0, n_rows) if row0 else n_rows, (256, 128))
    off = row0 // tm
    return pl.pallas_call(
        _final_norm_kernel,
        grid=(n_rows // tm,),
        in_specs=[
            pl.BlockSpec((tm, d), lambda i: (off + i, 0)),
            pl.BlockSpec((1, d), lambda i: (0, 0)),
        ],
        out_specs=pl.BlockSpec((tm, d), lambda i: (i, 0)),
        out_shape=jax.ShapeDtypeStruct((n_rows, d), F32),
        compiler_params=_params("parallel"),
        name="final_rmsnorm",
    )(x, g.reshape(1, d))


def _mm_kernel(*refs, n_pairs, has_bias, relu2, has_res):
    it = iter(refs)
    lhs = [next(it) for _ in range(n_pairs)]
    rhs = [next(it) for _ in range(n_pairs)]
    bias_ref = next(it) if has_bias else None
    x_ref = next(it) if has_res else None
    gate_ref = next(it) if has_res else None
    o_ref = next(it)
    acc = None
    for a_ref, b_ref in zip(lhs, rhs):
        part = jnp.dot(a_ref[...], b_ref[...], preferred_element_type=F32)
        acc = part if acc is None else acc + part
    if has_bias:
        acc = acc + bias_ref[...]
    if relu2:
        acc = jnp.square(jnp.maximum(acc, 0.0))
    if has_res:
        acc = x_ref[...] + gate_ref[...] * acc
    o_ref[...] = acc.astype(o_ref.dtype)


def _mm(lhs, lhs_cols, rhs, rhs_specs, n_out, out_dtype, *, tm, tn, bias=None, relu2=False, res=None, name):
    m = lhs[0].shape[0]
    in_specs = [pl.BlockSpec((tm, w), lambda i, j, blk=blk: (i, blk)) for w, blk in lhs_cols]
    in_specs += list(rhs_specs)
    args = list(lhs) + list(rhs)
    if bias is not None:
        args.append(bias[0])
        in_specs.append(bias[1])
    if res is not None:
        x, gate, gate_spec = res
        args += [x, gate]
        in_specs += [pl.BlockSpec((tm, tn), lambda i, j: (i, j)), gate_spec]
    return pl.pallas_call(
        functools.partial(_mm_kernel, n_pairs=len(lhs), has_bias=bias is not None, relu2=relu2,
                          has_res=res is not None),
        grid=(m // tm, n_out // tn),
        in_specs=in_specs,
        out_specs=pl.BlockSpec((tm, tn), lambda i, j: (i, j)),
        out_shape=jax.ShapeDtypeStruct((m, n_out), out_dtype),
        compiler_params=_params("parallel", "parallel"),
        name=name,
    )(*args)


def _mm_kgrid_kernel(a_ref, b_ref, bias_ref, x_ref, gate_ref, o_ref, acc_ref):
    k = pl.program_id(2)

    @pl.when(k == 0)
    def _():
        acc_ref[...] = jnp.zeros_like(acc_ref)

    acc_ref[...] += jnp.dot(a_ref[...], b_ref[...], preferred_element_type=F32)

    @pl.when(k == pl.num_programs(2) - 1)
    def _():
        o_ref[...] = x_ref[...] + gate_ref[...] * (acc_ref[...] + bias_ref[...])


def _mm_kgrid_res(a, w, layer, bias, x, gate, gate_spec, *, tm, tn, tk, name):
    m, k = a.shape
    n = w.shape[2]
    return pl.pallas_call(
        _mm_kgrid_kernel,
        grid=(m // tm, n // tn, k // tk),
        in_specs=[
            pl.BlockSpec((tm, tk), lambda i, j, kk: (i, kk)),
            pl.BlockSpec((None, tk, tn), lambda i, j, kk: (layer, kk, j)),
            pl.BlockSpec((None, 1, tn), lambda i, j, kk: (layer, 0, j)),
            pl.BlockSpec((tm, tn), lambda i, j, kk: (i, j)),
            gate_spec,
        ],
        out_specs=pl.BlockSpec((tm, tn), lambda i, j, kk: (i, j)),
        out_shape=jax.ShapeDtypeStruct((m, n), F32),
        scratch_shapes=[pltpu.VMEM((tm, tn), F32)],
        compiler_params=_params("parallel", "parallel", "arbitrary"),
        name=name,
    )(a, w, bias, x, gate)


def _fourier_pos_kernel(ct_ref, st_ref, xc_ref, xs_ref, wf_ref, o_ref):
    y = jnp.dot(ct_ref[...], xc_ref[...], preferred_element_type=F32)
    y = y + jnp.dot(st_ref[...], xs_ref[...], preferred_element_type=F32)
    o_ref[...] = jnp.dot(y.astype(BF16), wf_ref[...], preferred_element_type=F32).astype(o_ref.dtype)


def _fourier_pos(xcs, ct, st_neg, wf_bd, layer, row0, n_seq, t):
    f_w = xcs.shape[1] // 2
    tq = _pick(t, (256, 128))
    seq0 = row0 // t
    return pl.pallas_call(
        _fourier_pos_kernel,
        grid=(n_seq, t // tq),
        in_specs=[
            pl.BlockSpec((tq, t), lambda b, j: (j, 0)),
            pl.BlockSpec((tq, t), lambda b, j: (j, 0)),
            pl.BlockSpec((t, f_w), lambda b, j: (seq0 + b, 0)),
            pl.BlockSpec((t, f_w), lambda b, j: (seq0 + b, 1)),
            pl.BlockSpec((None, f_w, f_w), lambda b, j: (layer, 0, 0)),
        ],
        out_specs=pl.BlockSpec((tq, f_w), lambda b, j: (b * (t // tq) + j, 0)),
        out_shape=jax.ShapeDtypeStruct((n_seq * t, f_w), BF16),
        compiler_params=_params("parallel", "parallel"),
        name=f"fourier_position_t{t}",
    )(ct, st_neg, xcs, xcs, wf_bd)


def _dft_mats(t):
    idx = jnp.arange(t, dtype=jnp.int32)
    ang = ((idx[:, None] * idx[None, :]) % t).astype(F32) * (2.0 * math.pi / t)
    s = t ** -0.5
    return jnp.cos(ang) * s, -jnp.sin(ang) * s


def _conv_kernel(a_ref, g_ref, ap_ref, gp_ref, an_ref, gn_ref, wdw_ref, bdw_ref, lng_ref, lnb_ref,
                 wpw_ref, o_ref, hext_ref, shift_ref, conv_ref, *, rows, tt, row_chunk):
    i = pl.program_id(0)
    n_ctx_tiles = rows.n_ctx // tt
    per_seq = rows.t_lat // tt
    pos = jnp.where(i < n_ctx_tiles, 0, jnp.maximum(i - n_ctx_tiles, 0) % per_seq)
    last = jnp.where(i < n_ctx_tiles, 0, per_seq - 1)

    def glu(a, g):
        return a[...].astype(F32) * jax.nn.sigmoid(g[...].astype(F32))

    halo = CONV_HALO
    hext_ref[0:halo, :] = jnp.where(pos > 0, glu(ap_ref, gp_ref), 0.0)
    hext_ref[halo:halo + tt, :] = glu(a_ref, g_ref)
    hext_ref[halo + tt:2 * halo + tt, :] = jnp.where(pos < last, glu(an_ref, gn_ref), 0.0)

    n_shift_rows = shift_ref.shape[1]
    for s in range(SUBLANES):
        shift_ref[s] = hext_ref[s:s + n_shift_rows, :]

    c = conv_ref.shape[1]
    first_tap = halo - CONV_K // 2

    def chunk(r, carry):
        r0 = pl.multiple_of(r * row_chunk, row_chunk)
        acc = jnp.broadcast_to(bdw_ref[...], (row_chunk, c))
        for k in range(CONV_K):
            off = first_tap + k
            start = pl.multiple_of(r0 + (off // SUBLANES) * SUBLANES, SUBLANES)
            acc = acc + wdw_ref[k:k + 1, :] * shift_ref[off % SUBLANES, pl.ds(start, row_chunk), :]
        conv_ref[pl.ds(r0, row_chunk), :] = acc
        return carry

    lax.fori_loop(0, tt // row_chunk, chunk, 0)

    h = conv_ref[...]
    mu = jnp.mean(h, axis=-1, keepdims=True)
    var = jnp.mean(jnp.square(h - mu), axis=-1, keepdims=True)
    y = (h - mu) * lax.rsqrt(var + EPS) * lng_ref[...] + lnb_ref[...]
    y = y * jax.nn.sigmoid(y)
    o_ref[...] = jnp.dot(y.astype(BF16), wpw_ref[...], preferred_element_type=F32).astype(o_ref.dtype)


def _conv(u, col_a, w_dw, b_dw, ln_g, ln_b, w_pw, layer, rows, seq_ctx):
    m = u.shape[0]
    c = w_pw.shape[1]
    tt = seq_ctx
    assert rows.t_lat % tt == 0 and rows.n_ctx % tt == 0 and tt % CONV_HALO == 0
    hb = tt // CONV_HALO
    n_hblocks = m // CONV_HALO
    cur = lambda col: pl.BlockSpec((tt, c), lambda i: (i, col))
    prev = lambda col: pl.BlockSpec((CONV_HALO, c), lambda i: (jnp.maximum(i * hb - 1, 0), col))
    nxt = lambda col: pl.BlockSpec((CONV_HALO, c), lambda i: (jnp.minimum((i + 1) * hb, n_hblocks - 1), col))
    vec = lambda: pl.BlockSpec((None, 1, c), lambda i: (layer, 0, 0))
    taps = w_dw.shape[1]
    return pl.pallas_call(
        functools.partial(_conv_kernel, rows=rows, tt=tt, row_chunk=16),
        grid=(m // tt,),
        in_specs=[
            cur(col_a), cur(col_a + 1), prev(col_a), prev(col_a + 1), nxt(col_a), nxt(col_a + 1),
            pl.BlockSpec((None, taps, c), lambda i: (layer, 0, 0)),
            vec(), vec(), vec(),
            pl.BlockSpec((None, c, c), lambda i: (layer, 0, 0)),
        ],
        out_specs=pl.BlockSpec((tt, c), lambda i: (i, 0)),
        out_shape=jax.ShapeDtypeStruct((m, c), BF16),
        scratch_shapes=[pltpu.VMEM((tt + 2 * CONV_HALO, c), F32),
                        pltpu.VMEM((SUBLANES, tt + 2 * CONV_HALO - SUBLANES, c), F32),
                        pltpu.VMEM((tt, c), F32)],
        compiler_params=_params("parallel"),
        name="conformer_conv",
    )(u, u, u, u, u, u, w_dw, b_dw.reshape(-1, 1, c), ln_g.reshape(-1, 1, c), ln_b.reshape(-1, 1, c), w_pw)


def _attend(qs, keys, vals, masks, sink_col):
    nt = (((1,), (1,)), ((), ()))
    scores = []
    for k, mask in zip(keys, masks):
        s = lax.dot_general(qs, k, nt, preferred_element_type=F32) * SCALE
        scores.append(s if mask is None else jnp.where(mask, s, NEG_INF))
    m = sink_col
    for s in scores:
        m = jnp.maximum(m, jnp.max(s, axis=-1, keepdims=True))
    den = jnp.exp(sink_col - m)
    out = None
    for s, v in zip(scores, vals):
        p = jnp.exp(s - m)
        den = den + jnp.sum(p, axis=-1, keepdims=True)
        pv = jnp.dot(p.astype(BF16), v, preferred_element_type=F32)
        out = pv if out is None else out + pv
    return out / den


def _sink_col(sink_ref, layer, head0, rows_per_head):
    return jnp.concatenate(
        [jnp.full((rows_per_head, 1), sink_ref[layer, head0 + g], F32) for g in range(GROUP)], axis=0)


def _attn_ctx_kernel(sink_ref, q_ref, k_ref, v_ref, o_ref, *, layer):
    kv = pl.program_id(1)
    t = q_ref.shape[0]
    qs = jnp.concatenate([q_ref[:, g * HEAD_DIM:(g + 1) * HEAD_DIM] for g in range(GROUP)], axis=0)
    out = _attend(qs, [k_ref[...].astype(BF16)], [v_ref[...].astype(BF16)], [None],
                  _sink_col(sink_ref, layer, kv * GROUP, t))
    for g in range(GROUP):
        o_ref[:, g * HEAD_DIM:(g + 1) * HEAD_DIM] = out[g * t:(g + 1) * t].astype(o_ref.dtype)


def _attn_ctx(u, q_col0, ukv, sink, layer, n_seq, t, n_kv):
    qw = GROUP * HEAD_DIM
    return pl.pallas_call(
        functools.partial(_attn_ctx_kernel, layer=layer),
        grid=(n_seq, n_kv),
        in_specs=[
            pl.BlockSpec(memory_space=pltpu.SMEM),
            pl.BlockSpec((t, qw), lambda b, kv: (b, q_col0 + kv)),
            pl.BlockSpec((t, HEAD_DIM), lambda b, kv: (b, kv)),
            pl.BlockSpec((t, HEAD_DIM), lambda b, kv: (b, n_kv + kv)),
        ],
        out_specs=pl.BlockSpec((t, qw), lambda b, kv: (b, kv)),
        out_shape=jax.ShapeDtypeStruct((n_seq * t, n_kv * qw), BF16),
        compiler_params=_params("parallel", "parallel"),
        name="attention_context",
    )(sink, u, ukv, ukv)


def _rope(x, cs):
    cos, sin = cs[:, :HEAD_DIM], cs[:, HEAD_DIM:]
    lane = lax.broadcasted_iota(jnp.int32, x.shape, 1)
    quarter = HEAD_DIM // 4
    partner = jnp.where((lane & quarter) == 0,
                        pltpu.roll(x, HEAD_DIM - quarter, axis=1), pltpu.roll(x, quarter, axis=1))
    return x * cos + partner * sin


def _attn_lat_kernel(sink_ref, q_ref, kvp_ref, kvc_ref, kvn_ref, csp_ref, csc_ref, csn_ref, ck_ref, cv_ref,
                     o_ref, *, layer, n_kv):
    i = pl.program_id(1)
    nb = pl.num_programs(1)
    kw = n_kv * HEAD_DIM
    rq = GROUP * BLK
    r = lax.broadcasted_iota(jnp.int32, (rq, 3 * BLK), 0) % BLK
    j = lax.broadcasted_iota(jnp.int32, (rq, 3 * BLK), 1)
    lo = jnp.where(i == 0, BLK, 0)
    hi = jnp.where(i == nb - 1, 2 * BLK, 3 * BLK)
    mask = (j >= r + (BLK - WINDOW)) & (j <= r + (BLK + WINDOW)) & (j >= lo) & (j < hi)
    cs_q = csc_ref[...]
    for kv in range(n_kv):
        hs = slice(kv * HEAD_DIM, (kv + 1) * HEAD_DIM)
        vs = slice(kw + kv * HEAD_DIM, kw + (kv + 1) * HEAD_DIM)
        keys = jnp.concatenate(
            [_rope(ref[:, hs], cs[...]).astype(BF16)
             for ref, cs in ((kvp_ref, csp_ref), (kvc_ref, csc_ref), (kvn_ref, csn_ref))], axis=0)
        vals = jnp.concatenate([ref[:, vs].astype(BF16) for ref in (kvp_ref, kvc_ref, kvn_ref)], axis=0)
        qs = jnp.concatenate(
            [_rope(q_ref[:, (kv * GROUP + g) * HEAD_DIM:(kv * GROUP + g + 1) * HEAD_DIM].astype(F32),
                   cs_q).astype(BF16) for g in range(GROUP)], axis=0)
        out = _attend(qs, [ck_ref[:, hs].astype(BF16), keys], [cv_ref[:, hs].astype(BF16), vals],
                      [None, mask], _sink_col(sink_ref, layer, kv * GROUP, BLK))
        for g in range(GROUP):
            h = kv * GROUP + g
            o_ref[:, h * HEAD_DIM:(h + 1) * HEAD_DIM] = out[g * BLK:(g + 1) * BLK].astype(o_ref.dtype)


def _attn_lat(u, q_col0, ukv, cs, cache_k, cache_v, sink, layer, row0, n_seq, t, n_kv):
    qw = n_kv * GROUP * HEAD_DIM
    kw = n_kv * HEAD_DIM
    nb = t // BLK
    blk0 = row0 // BLK
    past = cache_k.shape[2]
    kv_spec = lambda f: pl.BlockSpec((BLK, 2 * kw), lambda b, i: (blk0 + b * nb + f(i), 0))
    cs_spec = lambda f: pl.BlockSpec((BLK, 2 * HEAD_DIM), lambda b, i: (f(i), 0))
    prev = lambda i: jnp.maximum(i - 1, 0)
    cur = lambda i: i
    nxt = lambda i: jnp.minimum(i + 1, nb - 1)
    cache_spec = pl.BlockSpec((None, None, past, kw), lambda b, i: (b, layer, 0, 0))
    return pl.pallas_call(
        functools.partial(_attn_lat_kernel, layer=layer, n_kv=n_kv),
        grid=(n_seq, nb),
        in_specs=[
            pl.BlockSpec(memory_space=pltpu.SMEM),
            pl.BlockSpec((BLK, qw), lambda b, i: (blk0 + b * nb + i, q_col0)),
            kv_spec(prev), kv_spec(cur), kv_spec(nxt),
            cs_spec(prev), cs_spec(cur), cs_spec(nxt),
            cache_spec, cache_spec,
        ],
        out_specs=pl.BlockSpec((BLK, qw), lambda b, i: (b * nb + i, 0)),
        out_shape=jax.ShapeDtypeStruct((n_seq * t, qw), BF16),
        compiler_params=_params("parallel", "parallel"),
        name="attention_latent",
    )(sink, u, ukv, ukv, ukv, cs, cs, cs, cache_k, cache_v)


def _rope_table(t):
    rot_axis = HEAD_DIM // 2
    pos = jnp.arange(t)
    row = (pos // GRID_W).astype(F32)
    col = (pos % GRID_W).astype(F32)
    inv = ROPE_BASE ** (-jnp.arange(0, rot_axis, 2, dtype=F32) / rot_axis)
    ar, ac = row[:, None] * inv, col[:, None] * inv
    cos = jnp.concatenate([jnp.cos(ar), jnp.cos(ar), jnp.cos(ac), jnp.cos(ac)], axis=-1)
    sin = jnp.concatenate([-jnp.sin(ar), jnp.sin(ar), -jnp.sin(ac), jnp.sin(ac)], axis=-1)
    return jnp.concatenate([cos, sin], axis=-1)


def kernel(x_prompt, x_sample, cache_k, cache_v, c, c_ctx, norm1_g, norm2_g, w_ada, b_ada, w_in, w_fourier, w_dw, b_dw, conv_ln_g, conv_ln_b, w_pw, sink, w_out, w_mlp1, b_mlp1, w_mlp2, b_mlp2, final_g):
    batch, seq, d = x_prompt.shape
    dec_batch, dec_seq, _ = x_sample.shape
    depth = w_in.shape[0]
    n_heads = sink.shape[1]
    n_kv = n_heads // GROUP
    q_w, kv_w = n_heads * HEAD_DIM, n_kv * HEAD_DIM
    f_w = w_fourier.shape[1] * w_fourier.shape[2]
    conv_ch = w_pw.shape[1]
    in_cols = w_in.shape[2]
    d_ff = w_mlp1.shape[2]
    assert in_cols == f_w + 2 * conv_ch + q_w + 2 * kv_w and w_fourier.shape[1] == F_GROUPS
    assert f_w == conv_ch == GROUP * HEAD_DIM * 2 and q_w == 2 * f_w, "column blocks below assume these widths"
    assert w_dw.shape[1] == CONV_K and dec_seq % GRID_W == 0

    n_ctx, n_lat = batch * seq, dec_batch * dec_seq
    m = n_ctx + n_lat
    rows = _Rows(n_ctx, dec_seq)
    assert n_ctx % dec_seq == 0, "latent sequences must start on a multiple of their length"
    tm = _pick(math.gcd(n_ctx, dec_seq), (1024, 512, 256))
    tn = 1024
    main_cols = q_w + f_w + 2 * conv_ch
    q_split = f_w + 2 * conv_ch
    blk_fourier, blk_glu = q_w // f_w, q_w // f_w + 1

    cond_rows = -(-(1 + dec_batch) // 8) * 8
    cond = jnp.zeros((cond_rows, d), F32).at[0].set(c_ctx).at[1:1 + dec_batch].set(c)
    mod = _ada(cond, w_ada, b_ada).reshape(depth, cond_rows, 6, 1, d)
    shift_a, scale_a, gate_a, shift_m, scale_m, gate_m = range(6)

    w_in_main = jnp.concatenate(
        [w_in[:, :, q_split:q_split + q_w].astype(BF16), w_in[:, :, :q_split].astype(BF16)], axis=2)
    w_in_kv = w_in[:, :, q_split + q_w:].astype(BF16)
    w_out_b = w_out.astype(BF16)
    w1_b, w2_b, w_pw_b = w_mlp1.astype(BF16), w_mlp2.astype(BF16), w_pw.astype(BF16)
    f_gc = f_w // F_GROUPS
    eye_g = jnp.eye(F_GROUPS, dtype=F32)
    wf_bd = jnp.einsum("gh,lgcd->lgchd", eye_g, w_fourier).reshape(depth, f_w, f_w).astype(BF16)
    c_ch, s_ch_neg = _dft_mats(f_gc)
    chan = jnp.concatenate([jnp.kron(eye_g, c_ch), jnp.kron(eye_g, -s_ch_neg)], axis=1).astype(BF16)
    dft_ctx = [a.astype(BF16) for a in _dft_mats(seq)]
    dft_lat = [a.astype(BF16) for a in _dft_mats(dec_seq)]
    cs_lat = _rope_table(dec_seq)
    cache_k2 = cache_k.reshape(dec_batch, depth, cache_k.shape[2], kv_w)
    cache_v2 = cache_v.reshape(dec_batch, depth, cache_v.shape[2], kv_w)

    x = jnp.concatenate([x_prompt.reshape(n_ctx, d), x_sample.reshape(n_lat, d)], axis=0)
    gate_spec = lambda layer, which: pl.BlockSpec(
        (None, None, None, 1, tn), lambda i, j, *_: (layer, rows.mod_row(i, tm), which, 0, j))
    w_tile = lambda layer, k_rows, k_blk: pl.BlockSpec((None, k_rows, tn), lambda i, j: (layer, k_blk, j))
    b1 = b_mlp1.reshape(depth, 1, d_ff)
    b2 = b_mlp2.reshape(depth, 1, d)

    new_k, new_v = [], []
    for l in range(depth):
        n1 = _norm_mod(x, norm1_g, mod, l, scale_a, shift_a, rows)
        u = _mm([n1], [(d, 0)], [w_in_main], [w_tile(l, d, 0)], main_cols, BF16, tm=tm, tn=tn,
                name="in_proj_main")
        ukv = _mm([n1], [(d, 0)], [w_in_kv], [w_tile(l, d, 0)], 2 * kv_w, F32, tm=tm, tn=tn,
                  name="in_proj_kv")
        new_k.append(ukv[:n_ctx, :kv_w].reshape(batch, seq, n_kv, HEAD_DIM))
        new_v.append(ukv[:n_ctx, kv_w:].reshape(batch, seq, n_kv, HEAD_DIM))

        xcs = _mm([u], [(f_w, blk_fourier)], [chan], [pl.BlockSpec((f_w, tn), lambda i, j: (0, j))],
                  2 * f_w, BF16, tm=tm, tn=tn, name="fourier_channel")
        yf = jnp.concatenate([
            _fourier_pos(xcs, dft_ctx[0], dft_ctx[1], wf_bd, l, 0, batch, seq),
            _fourier_pos(xcs, dft_lat[0], dft_lat[1], wf_bd, l, n_ctx, dec_batch, dec_seq)], axis=0)

        yc = _conv(u, blk_glu, w_dw, b_dw, conv_ln_g, conv_ln_b, w_pw_b, l, rows, seq)

        att = jnp.concatenate([
            _attn_ctx(u, 0, ukv, sink, l, batch, seq, n_kv),
            _attn_lat(u, 0, ukv, cs_lat, cache_k2, cache_v2, sink, l, n_ctx, dec_batch, dec_seq, n_kv)],
            axis=0)

        x = _mm([yf, yc, att], [(f_w, 0), (conv_ch, 0), (q_w, 0)], [w_out_b, w_out_b, w_out_b],
                [w_tile(l, f_w, 0), w_tile(l, conv_ch, 1), w_tile(l, q_w, 1)],
                d, F32, tm=tm, tn=tn, res=(x, mod, gate_spec(l, gate_a)), name="out_proj")

        n2 = _norm_mod(x, norm2_g, mod, l, scale_m, shift_m, rows)
        h = _mm([n2], [(d, 0)], [w1_b], [w_tile(l, d, 0)], d_ff, BF16, tm=tm, tn=tn,
                bias=(b1, pl.BlockSpec((None, 1, tn), lambda i, j, l=l: (l, 0, j))),
                relu2=True, name="mlp_up")
        x = _mm_kgrid_res(h, w2_b, l, b2, x, mod, gate_spec(l, gate_m),
                          tm=tm, tn=tn, tk=min(2048, d_ff), name="mlp_down")

    y_prompt = _final_norm(x, final_g, 0, n_ctx).reshape(batch, seq, d)
    y_sample = _final_norm(x, final_g, n_ctx, n_lat).reshape(dec_batch, dec_seq, d)
    return (y_prompt, y_sample, jnp.stack(new_k, axis=1), jnp.stack(new_v, axis=1))
```

```python
import functools
import math

import jax
import jax.numpy as jnp
from jax import lax
from jax.experimental import pallas as pl
from jax.experimental.pallas import tpu as pltpu

GRID_W = 64
HEAD_DIM = 128
GROUP = 4
F_GROUPS = 8
CONV_K = 31
CONV_HALO = 16
SUBLANES = 8
WINDOW = 128
BLK = 128
ROPE_BASE = 10000.0
EPS = 1e-6
NEG_INF = -1e30
SCALE = HEAD_DIM ** -0.5

V7X_VMEM_BYTES = 64 * 1024 * 1024
VMEM_LIMIT_BYTES = V7X_VMEM_BYTES - 8 * 1024 * 1024

F32 = jnp.float32
BF16 = jnp.bfloat16


def _params(*semantics):
    return pltpu.CompilerParams(dimension_semantics=semantics, vmem_limit_bytes=VMEM_LIMIT_BYTES)


def _pick(n, candidates):
    for c in candidates:
        if n % c == 0:
            return c
    raise ValueError(f"no tile in {candidates} divides {n}")


class _Rows:
    def __init__(self, n_ctx, t_lat):
        self.n_ctx, self.t_lat = n_ctx, t_lat

    def mod_row(self, i, tm):
        n_ctx_tiles = self.n_ctx // tm
        per_seq = self.t_lat // tm
        return jnp.where(i < n_ctx_tiles, 0, 1 + jnp.maximum(i - n_ctx_tiles, 0) // per_seq)


def _split_bf16(v):
    hi = v.astype(BF16)
    return hi, (v - hi.astype(F32)).astype(BF16)


def _ada_kernel(c_ref, w_ref, b_ref, o_ref, *, k_chunk):
    acc = b_ref[...] + jnp.zeros(o_ref.shape, F32)
    for k0 in range(0, c_ref.shape[1], k_chunk):
        c = c_ref[:, k0:k0 + k_chunk]
        s_hi, s_lo = _split_bf16(c * jax.nn.sigmoid(c))
        w_hi, w_lo = _split_bf16(w_ref[k0:k0 + k_chunk, :])
        acc = acc + jnp.dot(s_hi, w_hi, preferred_element_type=F32)
        acc = acc + jnp.dot(s_lo, w_hi, preferred_element_type=F32)
        acc = acc + jnp.dot(s_hi, w_lo, preferred_element_type=F32)
    o_ref[...] = acc


def _ada(cond, w_ada, b_ada):
    depth, d, n = w_ada.shape
    rows = cond.shape[0]
    tn = _pick(n, (512, 256, 128))
    return pl.pallas_call(
        functools.partial(_ada_kernel, k_chunk=_pick(d, (512, 256, 128))),
        grid=(depth, n // tn),
        in_specs=[
            pl.BlockSpec((rows, d), lambda l, j: (0, 0)),
            pl.BlockSpec((None, d, tn), lambda l, j: (l, 0, j)),
            pl.BlockSpec((None, 1, tn), lambda l, j: (l, 0, j)),
        ],
        out_specs=pl.BlockSpec((None, rows, tn), lambda l, j: (l, 0, j)),
        out_shape=jax.ShapeDtypeStruct((depth, rows, n), F32),
        compiler_params=_params("parallel", "parallel"),
        name="ada_modulation",
    )(cond, w_ada, b_ada.reshape(depth, 1, n))


def _norm_mod_kernel(x_ref, g_ref, scale_ref, shift_ref, o_ref):
    x = x_ref[...]
    y = x * lax.rsqrt(jnp.mean(x * x, axis=-1, keepdims=True) + EPS) * g_ref[...]
    o_ref[...] = (y * (1.0 + scale_ref[...]) + shift_ref[...]).astype(o_ref.dtype)


def _norm_mod(x, g, mod, layer, which_scale, which_shift, rows):
    m, d = x.shape
    tm = _pick(math.gcd(rows.n_ctx, rows.t_lat), (512, 256, 128))
    mod_spec = lambda which: pl.BlockSpec(
        (None, None, None, 1, d), lambda i: (layer, rows.mod_row(i, tm), which, 0, 0))
    return pl.pallas_call(
        _norm_mod_kernel,
        grid=(m // tm,),
        in_specs=[
            pl.BlockSpec((tm, d), lambda i: (i, 0)),
            pl.BlockSpec((None, 1, d), lambda i: (layer, 0, 0)),
            mod_spec(which_scale),
            mod_spec(which_shift),
        ],
        out_specs=pl.BlockSpec((tm, d), lambda i: (i, 0)),
        out_shape=jax.ShapeDtypeStruct((m, d), BF16),
        compiler_params=_params("parallel"),
        name="rmsnorm_modulate",
    )(x, g.reshape(g.shape[0], 1, d), mod, mod)


def _final_norm_kernel(x_ref, g_ref, o_ref):
    x = x_ref[...]
    y = x * lax.rsqrt(jnp.mean(x * x, axis=-1, keepdims=True) + EPS)
    o_ref[...] = y * g_ref[...]


def _final_norm(x, g, row0, n_rows):
    d = x.shape[1]
    tm = _pick(math.gcd(row0, n_rows) if row0 else n_rows, (512, 256, 128))
    off = row0 // tm
    return pl.pallas_call(
        _final_norm_kernel,
        grid=(n_rows // tm,),
        in_specs=[
            pl.BlockSpec((tm, d), lambda i: (off + i, 0)),
            pl.BlockSpec((1, d), lambda i: (0, 0)),
        ],
        out_specs=pl.BlockSpec((tm, d), lambda i: (i, 0)),
        out_shape=jax.ShapeDtypeStruct((n_rows, d), F32),
        compiler_params=_params("parallel"),
        name="final_rmsnorm",
    )(x, g.reshape(1, d))


def _mm_kernel(*refs, n_pairs, has_bias, relu2, has_res):
    it = iter(refs)
    lhs = [next(it) for _ in range(n_pairs)]
    rhs = [next(it) for _ in range(n_pairs)]
    bias_ref = next(it) if has_bias else None
    x_ref = next(it) if has_res else None
    gate_ref = next(it) if has_res else None
    o_ref = next(it)
    acc = None
    for a_ref, b_ref in zip(lhs, rhs):
        part = jnp.dot(a_ref[...], b_ref[...], preferred_element_type=F32)
        acc = part if acc is None else acc + part
    if has_bias:
        acc = acc + bias_ref[...]
    if relu2:
        acc = jnp.square(jnp.maximum(acc, 0.0))
    if has_res:
        acc = x_ref[...] + gate_ref[...] * acc
    o_ref[...] = acc.astype(o_ref.dtype)


def _mm(lhs, lhs_cols, rhs, rhs_specs, n_out, out_dtype, *, tm, tn, bias=None, relu2=False, res=None, name):
    m = lhs[0].shape[0]
    in_specs = [pl.BlockSpec((tm, w), lambda i, j, blk=blk: (i, blk)) for w, blk in lhs_cols]
    in_specs += list(rhs_specs)
    args = list(lhs) + list(rhs)
    if bias is not None:
        args.append(bias[0])
        in_specs.append(bias[1])
    if res is not None:
        x, gate, gate_spec = res
        args += [x, gate]
        in_specs += [pl.BlockSpec((tm, tn), lambda i, j: (i, j)), gate_spec]
    return pl.pallas_call(
        functools.partial(_mm_kernel, n_pairs=len(lhs), has_bias=bias is not None, relu2=relu2,
                          has_res=res is not None),
        grid=(m // tm, n_out // tn),
        in_specs=in_specs,
        out_specs=pl.BlockSpec((tm, tn), lambda i, j: (i, j)),
        out_shape=jax.ShapeDtypeStruct((m, n_out), out_dtype),
        compiler_params=_params("parallel", "parallel"),
        name=name,
    )(*args)


def _mm_kgrid_kernel(a_ref, b_ref, bias_ref, x_ref, gate_ref, o_ref, acc_ref):
    k = pl.program_id(2)

    @pl.when(k == 0)
    def _():
        acc_ref[...] = jnp.zeros_like(acc_ref)

    acc_ref[...] += jnp.dot(a_ref[...], b_ref[...], preferred_element_type=F32)

    @pl.when(k == pl.num_programs(2) - 1)
    def _():
        o_ref[...] = x_ref[...] + gate_ref[...] * (acc_ref[...] + bias_ref[...])


def _mm_kgrid_res(a, w, layer, bias, x, gate, gate_spec, *, tm, tn, tk, name):
    m, k = a.shape
    n = w.shape[2]
    return pl.pallas_call(
        _mm_kgrid_kernel,
        grid=(m // tm, n // tn, k // tk),
        in_specs=[
            pl.BlockSpec((tm, tk), lambda i, j, kk: (i, kk)),
            pl.BlockSpec((None, tk, tn), lambda i, j, kk: (layer, kk, j)),
            pl.BlockSpec((None, 1, tn), lambda i, j, kk: (layer, 0, j)),
            pl.BlockSpec((tm, tn), lambda i, j, kk: (i, j)),
            gate_spec,
        ],
        out_specs=pl.BlockSpec((tm, tn), lambda i, j, kk: (i, j)),
        out_shape=jax.ShapeDtypeStruct((m, n), F32),
        scratch_shapes=[pltpu.VMEM((tm, tn), F32)],
        compiler_params=_params("parallel", "parallel", "arbitrary"),
        name=name,
    )(a, w, bias, x, gate)


def _fourier_pos_kernel(ct_ref, st_ref, xc_ref, xs_ref, wf_ref, o_ref):
    y = jnp.dot(ct_ref[...], xc_ref[...], preferred_element_type=F32)
    y = y + jnp.dot(st_ref[...], xs_ref[...], preferred_element_type=F32)
    o_ref[...] = jnp.dot(y.astype(BF16), wf_ref[...], preferred_element_type=F32).astype(o_ref.dtype)


def _fourier_pos(xcs, ct, st_neg, wf_bd, layer, row0, n_seq, t):
    f_w = xcs.shape[1] // 2
    tq = _pick(t, (256, 128))
    seq0 = row0 // t
    return pl.pallas_call(
        _fourier_pos_kernel,
        grid=(n_seq, t // tq),
        in_specs=[
            pl.BlockSpec((tq, t), lambda b, j: (j, 0)),
            pl.BlockSpec((tq, t), lambda b, j: (j, 0)),
            pl.BlockSpec((t, f_w), lambda b, j: (seq0 + b, 0)),
            pl.BlockSpec((t, f_w), lambda b, j: (seq0 + b, 1)),
            pl.BlockSpec((None, f_w, f_w), lambda b, j: (layer, 0, 0)),
        ],
        out_specs=pl.BlockSpec((tq, f_w), lambda b, j: (b * (t // tq) + j, 0)),
        out_shape=jax.ShapeDtypeStruct((n_seq * t, f_w), BF16),
        compiler_params=_params("parallel", "parallel"),
        name=f"fourier_position_t{t}",
    )(ct, st_neg, xcs, xcs, wf_bd)


def _dft_mats(t):
    idx = jnp.arange(t, dtype=jnp.int32)
    ang = ((idx[:, None] * idx[None, :]) % t).astype(F32) * (2.0 * math.pi / t)
    s = t ** -0.5
    return jnp.cos(ang) * s, -jnp.sin(ang) * s


def _conv_kernel(a_ref, g_ref, ap_ref, gp_ref, an_ref, gn_ref, wdw_ref, bdw_ref, lng_ref, lnb_ref,
                 wpw_ref, o_ref, hext_ref, shift_ref, conv_ref, *, rows, tt, row_chunk, lane_chunk):
    i = pl.program_id(0)
    n_ctx_tiles = rows.n_ctx // tt
    per_seq = rows.t_lat // tt
    pos = jnp.where(i < n_ctx_tiles, 0, jnp.maximum(i - n_ctx_tiles, 0) % per_seq)
    last = jnp.where(i < n_ctx_tiles, 0, per_seq - 1)

    def glu(a, g):
        return a[...].astype(F32) * jax.nn.sigmoid(g[...].astype(F32))

    halo = CONV_HALO
    hext_ref[0:halo, :] = jnp.where(pos > 0, glu(ap_ref, gp_ref), 0.0)
    hext_ref[halo:halo + tt, :] = glu(a_ref, g_ref)
    hext_ref[halo + tt:2 * halo + tt, :] = jnp.where(pos < last, glu(an_ref, gn_ref), 0.0)

    n_shift_rows = shift_ref.shape[1]
    for s in range(SUBLANES):
        shift_ref[s] = hext_ref[s:s + n_shift_rows, :]

    c = conv_ref.shape[1]
    first_tap = halo - CONV_K // 2

    groups = row_chunk // SUBLANES

    def chunk(r, carry):
        r0 = pl.multiple_of(r * row_chunk, row_chunk)
        for l0 in range(0, c, lane_chunk):
            lanes = slice(l0, l0 + lane_chunk)
            accs = [bdw_ref[:, lanes]] * groups
            for s in range(SUBLANES):
                qs = [q for q in range(-(-CONV_K // SUBLANES) + 1) if 0 <= q * SUBLANES + s - first_tap < CONV_K]
                taps = {q: wdw_ref[q * SUBLANES + s - first_tap, :, lanes] for q in qs}
                for u in range(min(qs), max(qs) + groups):
                    start = pl.multiple_of(r0 + u * SUBLANES, SUBLANES)
                    h = shift_ref[s, pl.ds(start, SUBLANES), lanes]
                    for q in qs:
                        if 0 <= u - q < groups:
                            accs[u - q] = accs[u - q] + taps[q] * h
            for a, acc in enumerate(accs):
                conv_ref[pl.ds(pl.multiple_of(r0 + a * SUBLANES, SUBLANES), SUBLANES), lanes] = acc
        return carry

    lax.fori_loop(0, tt // row_chunk, chunk, 0)

    h = conv_ref[...]
    mu = jnp.mean(h, axis=-1, keepdims=True)
    var = jnp.mean(jnp.square(h - mu), axis=-1, keepdims=True)
    y = (h - mu) * lax.rsqrt(var + EPS) * lng_ref[...] + lnb_ref[...]
    y = y * jax.nn.sigmoid(y)
    o_ref[...] = jnp.dot(y.astype(BF16), wpw_ref[...], preferred_element_type=F32).astype(o_ref.dtype)


def _conv(u, col_a, w_dw, b_dw, ln_g, ln_b, w_pw, layer, rows, seq_ctx):
    m = u.shape[0]
    c = w_pw.shape[1]
    tt = seq_ctx
    assert rows.t_lat % tt == 0 and rows.n_ctx % tt == 0 and tt % CONV_HALO == 0
    hb = tt // CONV_HALO
    n_hblocks = m // CONV_HALO
    cur = lambda col: pl.BlockSpec((tt, c), lambda i: (i, col))
    prev = lambda col: pl.BlockSpec((CONV_HALO, c), lambda i: (jnp.maximum(i * hb - 1, 0), col))
    nxt = lambda col: pl.BlockSpec((CONV_HALO, c), lambda i: (jnp.minimum((i + 1) * hb, n_hblocks - 1), col))
    vec = lambda: pl.BlockSpec((None, 1, c), lambda i: (layer, 0, 0))
    depth, taps = w_dw.shape[:2]
    w_dw_rep = jnp.broadcast_to(w_dw[:, :, None, :], (depth, taps, SUBLANES, c))
    b_dw_rep = jnp.broadcast_to(b_dw[:, None, :], (depth, SUBLANES, c))
    return pl.pallas_call(
        functools.partial(_conv_kernel, rows=rows, tt=tt, row_chunk=32, lane_chunk=512),
        grid=(m // tt,),
        in_specs=[
            cur(col_a), cur(col_a + 1), prev(col_a), prev(col_a + 1), nxt(col_a), nxt(col_a + 1),
            pl.BlockSpec((None, taps, SUBLANES, c), lambda i: (layer, 0, 0, 0)),
            pl.BlockSpec((None, SUBLANES, c), lambda i: (layer, 0, 0)),
            vec(), vec(),
            pl.BlockSpec((None, c, c), lambda i: (layer, 0, 0)),
        ],
        out_specs=pl.BlockSpec((tt, c), lambda i: (i, 0)),
        out_shape=jax.ShapeDtypeStruct((m, c), BF16),
        scratch_shapes=[pltpu.VMEM((tt + 2 * CONV_HALO, c), F32),
                        pltpu.VMEM((SUBLANES, tt + 2 * CONV_HALO - SUBLANES, c), F32),
                        pltpu.VMEM((tt, c), F32)],
        compiler_params=_params("parallel"),
        name="conformer_conv",
    )(u, u, u, u, u, u, w_dw_rep, b_dw_rep, ln_g.reshape(-1, 1, c), ln_b.reshape(-1, 1, c), w_pw)


def _attend_t(qs, keys, vals, masks, sink_row):
    nt = (((1,), (1,)), ((), ()))
    tn = (((0,), (0,)), ((), ()))
    scores = []
    for k, mask in zip(keys, masks):
        s = lax.dot_general(k, qs, nt, preferred_element_type=F32) * SCALE
        scores.append(s if mask is None else jnp.where(mask, s, NEG_INF))
    m = sink_row
    for s in scores:
        m = jnp.maximum(m, jnp.max(s, axis=0, keepdims=True))
    den = jnp.exp(sink_row - m)
    out = None
    for s, v in zip(scores, vals):
        p = jnp.exp(s - m)
        den = den + jnp.sum(p, axis=0, keepdims=True)
        pv = lax.dot_general(v, p.astype(BF16), tn, preferred_element_type=F32)
        out = pv if out is None else out + pv
    return out / den


def _sink_row(sink_ref, layer, head0, queries_per_head):
    return jnp.concatenate(
        [jnp.full((1, queries_per_head), sink_ref[layer, head0 + g], F32) for g in range(GROUP)], axis=1)


def _attn_ctx_kernel(sink_ref, q_ref, k_ref, v_ref, o_ref, *, layer):
    kv = pl.program_id(1)
    t = q_ref.shape[0]
    qs = jnp.concatenate([q_ref[:, g * HEAD_DIM:(g + 1) * HEAD_DIM] for g in range(GROUP)], axis=0)
    out_t = _attend_t(qs, [k_ref[...].astype(BF16)], [v_ref[...].astype(BF16)], [None],
                      _sink_row(sink_ref, layer, kv * GROUP, t))
    for g in range(GROUP):
        o_ref[:, g * HEAD_DIM:(g + 1) * HEAD_DIM] = out_t[:, g * t:(g + 1) * t].T.astype(o_ref.dtype)


def _attn_ctx(u, q_col0, ukv, sink, layer, n_seq, t, n_kv):
    qw = GROUP * HEAD_DIM
    return pl.pallas_call(
        functools.partial(_attn_ctx_kernel, layer=layer),
        grid=(n_seq, n_kv),
        in_specs=[
            pl.BlockSpec(memory_space=pltpu.SMEM),
            pl.BlockSpec((t, qw), lambda b, kv: (b, q_col0 + kv)),
            pl.BlockSpec((t, HEAD_DIM), lambda b, kv: (b, kv)),
            pl.BlockSpec((t, HEAD_DIM), lambda b, kv: (b, n_kv + kv)),
        ],
        out_specs=pl.BlockSpec((t, qw), lambda b, kv: (b, kv)),
        out_shape=jax.ShapeDtypeStruct((n_seq * t, n_kv * qw), BF16),
        compiler_params=_params("parallel", "parallel"),
        name="attention_context",
    )(sink, u, ukv, ukv)


def _rope(x, cs):
    cos, sin = cs[:, :HEAD_DIM], cs[:, HEAD_DIM:]
    lane = lax.broadcasted_iota(jnp.int32, x.shape, 1)
    quarter = HEAD_DIM // 4
    partner = jnp.where((lane & quarter) == 0,
                        pltpu.roll(x, HEAD_DIM - quarter, axis=1), pltpu.roll(x, quarter, axis=1))
    return x * cos + partner * sin


def _attn_lat_kernel(sink_ref, q_ref, kvp_ref, kvc_ref, kvn_ref, csp_ref, csc_ref, csn_ref, ck_ref, cv_ref,
                     o_ref, *, layer, n_kv):
    i = pl.program_id(1)
    nb = pl.num_programs(1)
    kw = n_kv * HEAD_DIM
    rq = GROUP * BLK
    j = lax.broadcasted_iota(jnp.int32, (3 * BLK, rq), 0)
    r = lax.broadcasted_iota(jnp.int32, (3 * BLK, rq), 1) % BLK
    lo = jnp.where(i == 0, BLK, 0)
    hi = jnp.where(i == nb - 1, 2 * BLK, 3 * BLK)
    mask = (j >= r + (BLK - WINDOW)) & (j <= r + (BLK + WINDOW)) & (j >= lo) & (j < hi)
    cs_q = csc_ref[...]
    for kv in range(n_kv):
        hs = slice(kv * HEAD_DIM, (kv + 1) * HEAD_DIM)
        vs = slice(kw + kv * HEAD_DIM, kw + (kv + 1) * HEAD_DIM)
        keys = jnp.concatenate(
            [_rope(ref[:, hs], cs[...]).astype(BF16)
             for ref, cs in ((kvp_ref, csp_ref), (kvc_ref, csc_ref), (kvn_ref, csn_ref))], axis=0)
        vals = jnp.concatenate([ref[:, vs].astype(BF16) for ref in (kvp_ref, kvc_ref, kvn_ref)], axis=0)
        qs = jnp.concatenate(
            [_rope(q_ref[:, (kv * GROUP + g) * HEAD_DIM:(kv * GROUP + g + 1) * HEAD_DIM].astype(F32),
                   cs_q).astype(BF16) for g in range(GROUP)], axis=0)
        out_t = _attend_t(qs, [ck_ref[:, hs].astype(BF16), keys], [cv_ref[:, hs].astype(BF16), vals],
                          [None, mask], _sink_row(sink_ref, layer, kv * GROUP, BLK))
        for g in range(GROUP):
            h = kv * GROUP + g
            o_ref[:, h * HEAD_DIM:(h + 1) * HEAD_DIM] = out_t[:, g * BLK:(g + 1) * BLK].T.astype(o_ref.dtype)


def _attn_lat(u, q_col0, ukv, cs, cache_k, cache_v, sink, layer, row0, n_seq, t, n_kv):
    qw = n_kv * GROUP * HEAD_DIM
    kw = n_kv * HEAD_DIM
    nb = t // BLK
    blk0 = row0 // BLK
    past = cache_k.shape[2]
    kv_spec = lambda f: pl.BlockSpec((BLK, 2 * kw), lambda b, i: (blk0 + b * nb + f(i), 0))
    cs_spec = lambda f: pl.BlockSpec((BLK, 2 * HEAD_DIM), lambda b, i: (f(i), 0))
    prev = lambda i: jnp.maximum(i - 1, 0)
    cur = lambda i: i
    nxt = lambda i: jnp.minimum(i + 1, nb - 1)
    cache_spec = pl.BlockSpec((None, None, past, kw), lambda b, i: (b, layer, 0, 0))
    return pl.pallas_call(
        functools.partial(_attn_lat_kernel, layer=layer, n_kv=n_kv),
        grid=(n_seq, nb),
        in_specs=[
            pl.BlockSpec(memory_space=pltpu.SMEM),
            pl.BlockSpec((BLK, qw), lambda b, i: (blk0 + b * nb + i, q_col0)),
            kv_spec(prev), kv_spec(cur), kv_spec(nxt),
            cs_spec(prev), cs_spec(cur), cs_spec(nxt),
            cache_spec, cache_spec,
        ],
        out_specs=pl.BlockSpec((BLK, qw), lambda b, i: (b * nb + i, 0)),
        out_shape=jax.ShapeDtypeStruct((n_seq * t, qw), BF16),
        compiler_params=_params("parallel", "parallel"),
        name="attention_latent",
    )(sink, u, ukv, ukv, ukv, cs, cs, cs, cache_k, cache_v)


def _rope_table(t):
    rot_axis = HEAD_DIM // 2
    pos = jnp.arange(t)
    row = (pos // GRID_W).astype(F32)
    col = (pos % GRID_W).astype(F32)
    inv = ROPE_BASE ** (-jnp.arange(0, rot_axis, 2, dtype=F32) / rot_axis)
    ar, ac = row[:, None] * inv, col[:, None] * inv
    cos = jnp.concatenate([jnp.cos(ar), jnp.cos(ar), jnp.cos(ac), jnp.cos(ac)], axis=-1)
    sin = jnp.concatenate([-jnp.sin(ar), jnp.sin(ar), -jnp.sin(ac), jnp.sin(ac)], axis=-1)
    return jnp.concatenate([cos, sin], axis=-1)


def kernel(x_prompt, x_sample, cache_k, cache_v, c, c_ctx, norm1_g, norm2_g, w_ada, b_ada, w_in, w_fourier, w_dw, b_dw, conv_ln_g, conv_ln_b, w_pw, sink, w_out, w_mlp1, b_mlp1, w_mlp2, b_mlp2, final_g):
    batch, seq, d = x_prompt.shape
    dec_batch, dec_seq, _ = x_sample.shape
    depth = w_in.shape[0]
    n_heads = sink.shape[1]
    n_kv = n_heads // GROUP
    q_w, kv_w = n_heads * HEAD_DIM, n_kv * HEAD_DIM
    f_w = w_fourier.shape[1] * w_fourier.shape[2]
    conv_ch = w_pw.shape[1]
    in_cols = w_in.shape[2]
    d_ff = w_mlp1.shape[2]
    assert in_cols == f_w + 2 * conv_ch + q_w + 2 * kv_w and w_fourier.shape[1] == F_GROUPS
    assert f_w == conv_ch == GROUP * HEAD_DIM * 2 and q_w == 2 * f_w, "column blocks below assume these widths"
    assert w_dw.shape[1] == CONV_K and dec_seq % GRID_W == 0

    n_ctx, n_lat = batch * seq, dec_batch * dec_seq
    m = n_ctx + n_lat
    rows = _Rows(n_ctx, dec_seq)
    assert n_ctx % dec_seq == 0, "latent sequences must start on a multiple of their length"
    tm = _pick(math.gcd(n_ctx, dec_seq), (1024, 512, 256))
    tn = 1024
    main_cols = q_w + f_w + 2 * conv_ch
    q_split = f_w + 2 * conv_ch
    blk_fourier, blk_glu = q_w // f_w, q_w // f_w + 1

    cond_rows = -(-(1 + dec_batch) // 8) * 8
    cond = jnp.zeros((cond_rows, d), F32).at[0].set(c_ctx).at[1:1 + dec_batch].set(c)
    mod = _ada(cond, w_ada, b_ada).reshape(depth, cond_rows, 6, 1, d)
    shift_a, scale_a, gate_a, shift_m, scale_m, gate_m = range(6)

    w_in_main = jnp.concatenate(
        [w_in[:, :, q_split:q_split + q_w].astype(BF16), w_in[:, :, :q_split].astype(BF16)], axis=2)
    w_in_kv = w_in[:, :, q_split + q_w:].astype(BF16)
    w_out_b = w_out.astype(BF16)
    w1_b, w2_b, w_pw_b = w_mlp1.astype(BF16), w_mlp2.astype(BF16), w_pw.astype(BF16)
    f_gc = f_w // F_GROUPS
    eye_g = jnp.eye(F_GROUPS, dtype=F32)
    wf_bd = jnp.einsum("gh,lgcd->lgchd", eye_g, w_fourier).reshape(depth, f_w, f_w).astype(BF16)
    c_ch, s_ch_neg = _dft_mats(f_gc)
    chan = jnp.concatenate([jnp.kron(eye_g, c_ch), jnp.kron(eye_g, -s_ch_neg)], axis=1).astype(BF16)
    dft_ctx = [a.astype(BF16) for a in _dft_mats(seq)]
    dft_lat = [a.astype(BF16) for a in _dft_mats(dec_seq)]
    cs_lat = _rope_table(dec_seq)
    cache_k2 = cache_k.reshape(dec_batch, depth, cache_k.shape[2], kv_w)
    cache_v2 = cache_v.reshape(dec_batch, depth, cache_v.shape[2], kv_w)

    x = jnp.concatenate([x_prompt.reshape(n_ctx, d), x_sample.reshape(n_lat, d)], axis=0)
    gate_spec = lambda layer, which: pl.BlockSpec(
        (None, None, None, 1, tn), lambda i, j, *_: (layer, rows.mod_row(i, tm), which, 0, j))
    w_tile = lambda layer, k_rows, k_blk: pl.BlockSpec((None, k_rows, tn), lambda i, j: (layer, k_blk, j))
    b1 = b_mlp1.reshape(depth, 1, d_ff)
    b2 = b_mlp2.reshape(depth, 1, d)

    new_k, new_v = [], []
    for l in range(depth):
        n1 = _norm_mod(x, norm1_g, mod, l, scale_a, shift_a, rows)
        u = _mm([n1], [(d, 0)], [w_in_main], [w_tile(l, d, 0)], main_cols, BF16, tm=tm, tn=tn,
                name="in_proj_main")
        ukv = _mm([n1], [(d, 0)], [w_in_kv], [w_tile(l, d, 0)], 2 * kv_w, F32, tm=tm, tn=tn,
                  name="in_proj_kv")
        new_k.append(ukv[:n_ctx, :kv_w].reshape(batch, seq, n_kv, HEAD_DIM))
        new_v.append(ukv[:n_ctx, kv_w:].reshape(batch, seq, n_kv, HEAD_DIM))

        xcs = _mm([u], [(f_w, blk_fourier)], [chan], [pl.BlockSpec((f_w, tn), lambda i, j: (0, j))],
                  2 * f_w, BF16, tm=tm, tn=tn, name="fourier_channel")
        yf = jnp.concatenate([
            _fourier_pos(xcs, dft_ctx[0], dft_ctx[1], wf_bd, l, 0, batch, seq),
            _fourier_pos(xcs, dft_lat[0], dft_lat[1], wf_bd, l, n_ctx, dec_batch, dec_seq)], axis=0)

        yc = _conv(u, blk_glu, w_dw, b_dw, conv_ln_g, conv_ln_b, w_pw_b, l, rows, seq)

        att = jnp.concatenate([
            _attn_ctx(u, 0, ukv, sink, l, batch, seq, n_kv),
            _attn_lat(u, 0, ukv, cs_lat, cache_k2, cache_v2, sink, l, n_ctx, dec_batch, dec_seq, n_kv)],
            axis=0)

        x = _mm([yf, yc, att], [(f_w, 0), (conv_ch, 0), (q_w, 0)], [w_out_b, w_out_b, w_out_b],
                [w_tile(l, f_w, 0), w_tile(l, conv_ch, 1), w_tile(l, q_w, 1)],
                d, F32, tm=tm, tn=tn, res=(x, mod, gate_spec(l, gate_a)), name="out_proj")

        n2 = _norm_mod(x, norm2_g, mod, l, scale_m, shift_m, rows)
        h = _mm([n2], [(d, 0)], [w1_b], [w_tile(l, d, 0)], d_ff, BF16, tm=tm, tn=tn,
                bias=(b1, pl.BlockSpec((None, 1, tn), lambda i, j, l=l: (l, 0, j))),
                relu2=True, name="mlp_up")
        x = _mm_kgrid_res(h, w2_b, l, b2, x, mod, gate_spec(l, gate_m),
                          tm=tm, tn=tn, tk=min(2048, d_ff), name="mlp_down")

    y_prompt = _final_norm(x, final_g, 0, n_ctx).reshape(batch, seq, d)
    y_sample = _final_norm(x, final_g, n_ctx, n_lat).reshape(dec_batch, dec_seq, d)
    return (y_prompt, y_sample, jnp.stack(new_k, axis=1), jnp.stack(new_v, axis=1))
```

```python
import functools
import math

import jax
import jax.numpy as jnp
from jax import lax
from jax.experimental import pallas as pl
from jax.experimental.pallas import tpu as pltpu

GRID_W = 64
HEAD_DIM = 128
GROUP = 4
F_GROUPS = 8
CONV_K = 31
CONV_HALO = 16
SUBLANES = 8
DFT_EXTRA_ROWS = 16
WINDOW = 128
BLK = 128
ROPE_BASE = 10000.0
EPS = 1e-6
NEG_INF = -1e30
SCALE = HEAD_DIM ** -0.5

V7X_VMEM_BYTES = 64 * 1024 * 1024
VMEM_LIMIT_BYTES = V7X_VMEM_BYTES - 8 * 1024 * 1024

F32 = jnp.float32
BF16 = jnp.bfloat16


def _params(*semantics):
    return pltpu.CompilerParams(dimension_semantics=semantics, vmem_limit_bytes=VMEM_LIMIT_BYTES)


def _pick(n, candidates):
    for c in candidates:
        if n % c == 0:
            return c
    raise ValueError(f"no tile in {candidates} divides {n}")


class _Rows:
    def __init__(self, n_ctx, t_lat):
        self.n_ctx, self.t_lat = n_ctx, t_lat

    def mod_row(self, i, tm):
        n_ctx_tiles = self.n_ctx // tm
        per_seq = self.t_lat // tm
        return jnp.where(i < n_ctx_tiles, 0, 1 + jnp.maximum(i - n_ctx_tiles, 0) // per_seq)


def _split_bf16(v):
    hi = v.astype(BF16)
    return hi, (v - hi.astype(F32)).astype(BF16)


def _ada_kernel(c_ref, w_ref, b_ref, o_ref, *, k_chunk):
    acc = b_ref[...] + jnp.zeros(o_ref.shape, F32)
    for k0 in range(0, c_ref.shape[1], k_chunk):
        c = c_ref[:, k0:k0 + k_chunk]
        s_hi, s_lo = _split_bf16(c * jax.nn.sigmoid(c))
        w_hi, w_lo = _split_bf16(w_ref[k0:k0 + k_chunk, :])
        acc = acc + jnp.dot(s_hi, w_hi, preferred_element_type=F32)
        acc = acc + jnp.dot(s_lo, w_hi, preferred_element_type=F32)
        acc = acc + jnp.dot(s_hi, w_lo, preferred_element_type=F32)
    o_ref[...] = acc


def _ada(cond, w_ada, b_ada):
    depth, d, n = w_ada.shape
    rows = cond.shape[0]
    tn = _pick(n, (512, 256, 128))
    return pl.pallas_call(
        functools.partial(_ada_kernel, k_chunk=_pick(d, (512, 256, 128))),
        grid=(depth, n // tn),
        in_specs=[
            pl.BlockSpec((rows, d), lambda l, j: (0, 0)),
            pl.BlockSpec((None, d, tn), lambda l, j: (l, 0, j)),
            pl.BlockSpec((None, 1, tn), lambda l, j: (l, 0, j)),
        ],
        out_specs=pl.BlockSpec((None, rows, tn), lambda l, j: (l, 0, j)),
        out_shape=jax.ShapeDtypeStruct((depth, rows, n), F32),
        compiler_params=_params("parallel", "parallel"),
        name="ada_modulation",
    )(cond, w_ada, b_ada.reshape(depth, 1, n))


def _norm_mod_kernel(x_ref, g_ref, scale_ref, shift_ref, o_ref):
    x = x_ref[...]
    y = x * lax.rsqrt(jnp.mean(x * x, axis=-1, keepdims=True) + EPS) * g_ref[...]
    o_ref[...] = (y * (1.0 + scale_ref[...]) + shift_ref[...]).astype(o_ref.dtype)


def _norm_mod(x, g, mod, layer, which_scale, which_shift, rows):
    m, d = x.shape
    tm = _pick(math.gcd(rows.n_ctx, rows.t_lat), (512, 256, 128))
    mod_spec = lambda which: pl.BlockSpec(
        (None, None, None, 1, d), lambda i: (layer, rows.mod_row(i, tm), which, 0, 0))
    return pl.pallas_call(
        _norm_mod_kernel,
        grid=(m // tm,),
        in_specs=[
            pl.BlockSpec((tm, d), lambda i: (i, 0)),
            pl.BlockSpec((None, 1, d), lambda i: (layer, 0, 0)),
            mod_spec(which_scale),
            mod_spec(which_shift),
        ],
        out_specs=pl.BlockSpec((tm, d), lambda i: (i, 0)),
        out_shape=jax.ShapeDtypeStruct((m, d), BF16),
        compiler_params=_params("parallel"),
        name="rmsnorm_modulate",
    )(x, g.reshape(g.shape[0], 1, d), mod, mod)


def _final_norm_kernel(x_ref, g_ref, o_ref):
    x = x_ref[...]
    y = x * lax.rsqrt(jnp.mean(x * x, axis=-1, keepdims=True) + EPS)
    o_ref[...] = y * g_ref[...]


def _final_norm(x, g, row0, n_rows):
    d = x.shape[1]
    tm = _pick(math.gcd(row0, n_rows) if row0 else n_rows, (512, 256, 128))
    off = row0 // tm
    return pl.pallas_call(
        _final_norm_kernel,
        grid=(n_rows // tm,),
        in_specs=[
            pl.BlockSpec((tm, d), lambda i: (off + i, 0)),
            pl.BlockSpec((1, d), lambda i: (0, 0)),
        ],
        out_specs=pl.BlockSpec((tm, d), lambda i: (i, 0)),
        out_shape=jax.ShapeDtypeStruct((n_rows, d), F32),
        compiler_params=_params("parallel"),
        name="final_rmsnorm",
    )(x, g.reshape(1, d))


def _mm_kernel(*refs, n_pairs, has_bias, relu2, has_res):
    it = iter(refs)
    lhs = [next(it) for _ in range(n_pairs)]
    rhs = [next(it) for _ in range(n_pairs)]
    bias_ref = next(it) if has_bias else None
    x_ref = next(it) if has_res else None
    gate_ref = next(it) if has_res else None
    o_ref = next(it)
    acc = None
    for a_ref, b_ref in zip(lhs, rhs):
        part = jnp.dot(a_ref[...], b_ref[...], preferred_element_type=F32)
        acc = part if acc is None else acc + part
    if has_bias:
        acc = acc + bias_ref[...]
    if relu2:
        acc = jnp.square(jnp.maximum(acc, 0.0))
    if has_res:
        acc = x_ref[...] + gate_ref[...] * acc
    o_ref[...] = acc.astype(o_ref.dtype)


def _mm(lhs, lhs_cols, rhs, rhs_specs, n_out, out_dtype, *, tm, tn, bias=None, relu2=False, res=None, name):
    m = lhs[0].shape[0]
    in_specs = [pl.BlockSpec((tm, w), lambda i, j, blk=blk: (i, blk)) for w, blk in lhs_cols]
    in_specs += list(rhs_specs)
    args = list(lhs) + list(rhs)
    if bias is not None:
        args.append(bias[0])
        in_specs.append(bias[1])
    if res is not None:
        x, gate, gate_spec = res
        args += [x, gate]
        in_specs += [pl.BlockSpec((tm, tn), lambda i, j: (i, j)), gate_spec]
    return pl.pallas_call(
        functools.partial(_mm_kernel, n_pairs=len(lhs), has_bias=bias is not None, relu2=relu2,
                          has_res=res is not None),
        grid=(m // tm, n_out // tn),
        in_specs=in_specs,
        out_specs=pl.BlockSpec((tm, tn), lambda i, j: (i, j)),
        out_shape=jax.ShapeDtypeStruct((m, n_out), out_dtype),
        compiler_params=_params("parallel", "parallel"),
        name=name,
    )(*args)


def _mm_kgrid_kernel(a_ref, b_ref, bias_ref, x_ref, gate_ref, o_ref, acc_ref):
    k = pl.program_id(2)

    @pl.when(k == 0)
    def _():
        acc_ref[...] = jnp.zeros_like(acc_ref)

    acc_ref[...] += jnp.dot(a_ref[...], b_ref[...], preferred_element_type=F32)

    @pl.when(k == pl.num_programs(2) - 1)
    def _():
        o_ref[...] = x_ref[...] + gate_ref[...] * (acc_ref[...] + bias_ref[...])


def _mm_kgrid_res(a, w, layer, bias, x, gate, gate_spec, *, tm, tn, tk, name):
    m, k = a.shape
    n = w.shape[2]
    return pl.pallas_call(
        _mm_kgrid_kernel,
        grid=(m // tm, n // tn, k // tk),
        in_specs=[
            pl.BlockSpec((tm, tk), lambda i, j, kk: (i, kk)),
            pl.BlockSpec((None, tk, tn), lambda i, j, kk: (layer, kk, j)),
            pl.BlockSpec((None, 1, tn), lambda i, j, kk: (layer, 0, j)),
            pl.BlockSpec((tm, tn), lambda i, j, kk: (i, j)),
            gate_spec,
        ],
        out_specs=pl.BlockSpec((tm, tn), lambda i, j, kk: (i, j)),
        out_shape=jax.ShapeDtypeStruct((m, n), F32),
        scratch_shapes=[pltpu.VMEM((tm, tn), F32)],
        compiler_params=_params("parallel", "parallel", "arbitrary"),
        name=name,
    )(a, w, bias, x, gate)


def _fourier_chan_kernel(x_ref, cs_ref, o_ref):
    f_gc = cs_ref.shape[0]
    f_w = x_ref.shape[1]
    for g in range(f_w // f_gc):
        cols = slice(g * f_gc, (g + 1) * f_gc)
        r = jnp.dot(x_ref[:, cols], cs_ref[...], preferred_element_type=F32)
        o_ref[:, cols] = r[:, :f_gc].astype(o_ref.dtype)
        o_ref[:, f_w + g * f_gc:f_w + (g + 1) * f_gc] = r[:, f_gc:].astype(o_ref.dtype)


def _fourier_chan(u, col_blk, chan_cs, f_w, tm):
    m = u.shape[0]
    return pl.pallas_call(
        _fourier_chan_kernel,
        grid=(m // tm,),
        in_specs=[
            pl.BlockSpec((tm, f_w), lambda i: (i, col_blk)),
            pl.BlockSpec(chan_cs.shape, lambda i: (0, 0)),
        ],
        out_specs=pl.BlockSpec((tm, 2 * f_w), lambda i: (i, 0)),
        out_shape=jax.ShapeDtypeStruct((m, 2 * f_w), BF16),
        compiler_params=_params("parallel"),
        name="fourier_channel",
    )(u, chan_cs)


def _fourier_pos_kernel(ct_ref, ctx_ref, st_ref, stx_ref, xc_ref, xs_ref, wf_ref, jrev_ref, lo_ref, hi_ref, d_ref):
    tq = ct_ref.shape[0]
    c_rows = jnp.concatenate([ct_ref[...], ctx_ref[...]], axis=0)
    s_rows = jnp.concatenate([st_ref[...], stx_ref[...]], axis=0)
    p = jnp.dot(c_rows, xc_ref[...], preferred_element_type=F32)
    q_neg = jnp.dot(s_rows, xs_ref[...], preferred_element_type=F32)
    d_ref[...] = p - q_neg
    lo = (p[:tq] + q_neg[:tq]).astype(BF16)
    up = d_ref[1:tq + 1, :].astype(BF16)
    f_gc = wf_ref.shape[1]
    z_up = []
    for g in range(wf_ref.shape[0]):
        cols = slice(g * f_gc, (g + 1) * f_gc)
        lo_ref[:, cols] = jnp.dot(lo[:, cols], wf_ref[g], preferred_element_type=F32).astype(lo_ref.dtype)
        z_up.append(jnp.dot(up[:, cols], wf_ref[g], preferred_element_type=F32).astype(BF16))
    hi_ref[...] = jnp.dot(jrev_ref[...], jnp.concatenate(z_up, axis=1),
                          preferred_element_type=F32).astype(hi_ref.dtype)


def _fourier_pos(xcs, dft, wf, layer, row0, n_seq, t):
    ct, st_neg, jrev = dft
    f_w = xcs.shape[1] // 2
    tq = jrev.shape[0]
    nj = t // (2 * tq)
    seq0 = row0 // t
    xb = DFT_EXTRA_ROWS
    main = lambda: pl.BlockSpec((tq, t), lambda b, j: (j, 0))
    extra = lambda: pl.BlockSpec((xb, t), lambda b, j: ((j + 1) * (tq // xb), 0))
    half = jax.ShapeDtypeStruct((n_seq * (t // 2), f_w), BF16)
    lo, hi = pl.pallas_call(
        _fourier_pos_kernel,
        grid=(n_seq, nj),
        in_specs=[
            main(), extra(), main(), extra(),
            pl.BlockSpec((t, f_w), lambda b, j: (seq0 + b, 0)),
            pl.BlockSpec((t, f_w), lambda b, j: (seq0 + b, 1)),
            pl.BlockSpec((None,) + wf.shape[1:], lambda b, j: (layer, 0, 0, 0)),
            pl.BlockSpec((tq, tq), lambda b, j: (0, 0)),
        ],
        out_specs=[pl.BlockSpec((tq, f_w), lambda b, j: (b * nj + j, 0)),
                   pl.BlockSpec((tq, f_w), lambda b, j: (b * nj + nj - 1 - j, 0))],
        out_shape=[half, half],
        scratch_shapes=[pltpu.VMEM((tq + xb, f_w), F32)],
        compiler_params=_params("parallel", "parallel"),
        name=f"fourier_position_t{t}",
    )(ct, ct, st_neg, st_neg, xcs, xcs, wf, jrev)
    return jnp.stack([lo.reshape(n_seq, t // 2, f_w), hi.reshape(n_seq, t // 2, f_w)], axis=1)


def _dft_tables(t):
    tq = _pick(t // 2, (256, 128))
    rows = jnp.arange(t // 2 + DFT_EXTRA_ROWS, dtype=jnp.int32)
    cols = jnp.arange(t, dtype=jnp.int32)
    ang = ((rows[:, None] * cols[None, :]) % t).astype(F32) * (2.0 * math.pi / t)
    s = t ** -0.5
    jrev = jnp.eye(tq, dtype=BF16)[::-1]
    return (jnp.cos(ang) * s).astype(BF16), (-jnp.sin(ang) * s).astype(BF16), jrev


def _conv_kernel(a_ref, g_ref, ap_ref, gp_ref, an_ref, gn_ref, wdw_ref, bdw_ref, lng_ref, lnb_ref,
                 wpw_ref, o_ref, hext_ref, shift_ref, conv_ref, *, rows, tt, row_chunk, lane_chunk):
    i = pl.program_id(0)
    n_ctx_tiles = rows.n_ctx // tt
    per_seq = rows.t_lat // tt
    pos = jnp.where(i < n_ctx_tiles, 0, jnp.maximum(i - n_ctx_tiles, 0) % per_seq)
    last = jnp.where(i < n_ctx_tiles, 0, per_seq - 1)

    def glu(a, g):
        return a[...].astype(F32) * jax.nn.sigmoid(g[...].astype(F32))

    halo = CONV_HALO
    hext_ref[0:halo, :] = jnp.where(pos > 0, glu(ap_ref, gp_ref), 0.0)
    hext_ref[halo:halo + tt, :] = glu(a_ref, g_ref)
    hext_ref[halo + tt:2 * halo + tt, :] = jnp.where(pos < last, glu(an_ref, gn_ref), 0.0)

    n_shift_rows = shift_ref.shape[1]
    for s in range(SUBLANES):
        shift_ref[s] = hext_ref[s:s + n_shift_rows, :]

    c = conv_ref.shape[1]
    first_tap = halo - CONV_K // 2

    groups = row_chunk // SUBLANES

    def chunk(r, carry):
        r0 = pl.multiple_of(r * row_chunk, row_chunk)
        for l0 in range(0, c, lane_chunk):
            lanes = slice(l0, l0 + lane_chunk)
            accs = [bdw_ref[:, lanes]] * groups
            for s in range(SUBLANES):
                qs = [q for q in range(-(-CONV_K // SUBLANES) + 1) if 0 <= q * SUBLANES + s - first_tap < CONV_K]
                taps = {q: wdw_ref[q * SUBLANES + s - first_tap, :, lanes] for q in qs}
                for u in range(min(qs), max(qs) + groups):
                    start = pl.multiple_of(r0 + u * SUBLANES, SUBLANES)
                    h = shift_ref[s, pl.ds(start, SUBLANES), lanes]
                    for q in qs:
                        if 0 <= u - q < groups:
                            accs[u - q] = accs[u - q] + taps[q] * h
            for a, acc in enumerate(accs):
                conv_ref[pl.ds(pl.multiple_of(r0 + a * SUBLANES, SUBLANES), SUBLANES), lanes] = acc
        return carry

    lax.fori_loop(0, tt // row_chunk, chunk, 0)

    h = conv_ref[...]
    mu = jnp.mean(h, axis=-1, keepdims=True)
    var = jnp.mean(jnp.square(h - mu), axis=-1, keepdims=True)
    y = (h - mu) * lax.rsqrt(var + EPS) * lng_ref[...] + lnb_ref[...]
    y = y * jax.nn.sigmoid(y)
    o_ref[...] = jnp.dot(y.astype(BF16), wpw_ref[...], preferred_element_type=F32).astype(o_ref.dtype)


def _conv(u, col_a, w_dw, b_dw, ln_g, ln_b, w_pw, layer, rows, seq_ctx):
    m = u.shape[0]
    c = w_pw.shape[1]
    tt = seq_ctx
    assert rows.t_lat % tt == 0 and rows.n_ctx % tt == 0 and tt % CONV_HALO == 0
    hb = tt // CONV_HALO
    n_hblocks = m // CONV_HALO
    cur = lambda col: pl.BlockSpec((tt, c), lambda i: (i, col))
    prev = lambda col: pl.BlockSpec((CONV_HALO, c), lambda i: (jnp.maximum(i * hb - 1, 0), col))
    nxt = lambda col: pl.BlockSpec((CONV_HALO, c), lambda i: (jnp.minimum((i + 1) * hb, n_hblocks - 1), col))
    vec = lambda: pl.BlockSpec((None, 1, c), lambda i: (layer, 0, 0))
    depth, taps = w_dw.shape[:2]
    w_dw_rep = jnp.broadcast_to(w_dw[:, :, None, :], (depth, taps, SUBLANES, c))
    b_dw_rep = jnp.broadcast_to(b_dw[:, None, :], (depth, SUBLANES, c))
    return pl.pallas_call(
        functools.partial(_conv_kernel, rows=rows, tt=tt, row_chunk=32, lane_chunk=512),
        grid=(m // tt,),
        in_specs=[
            cur(col_a), cur(col_a + 1), prev(col_a), prev(col_a + 1), nxt(col_a), nxt(col_a + 1),
            pl.BlockSpec((None, taps, SUBLANES, c), lambda i: (layer, 0, 0, 0)),
            pl.BlockSpec((None, SUBLANES, c), lambda i: (layer, 0, 0)),
            vec(), vec(),
            pl.BlockSpec((None, c, c), lambda i: (layer, 0, 0)),
        ],
        out_specs=pl.BlockSpec((tt, c), lambda i: (i, 0)),
        out_shape=jax.ShapeDtypeStruct((m, c), BF16),
        scratch_shapes=[pltpu.VMEM((tt + 2 * CONV_HALO, c), F32),
                        pltpu.VMEM((SUBLANES, tt + 2 * CONV_HALO - SUBLANES, c), F32),
                        pltpu.VMEM((tt, c), F32)],
        compiler_params=_params("parallel"),
        name="conformer_conv",
    )(u, u, u, u, u, u, w_dw_rep, b_dw_rep, ln_g.reshape(-1, 1, c), ln_b.reshape(-1, 1, c), w_pw)


def _attend_t(qs, keys, vals, masks, sink_row):
    nt = (((1,), (1,)), ((), ()))
    tn = (((0,), (0,)), ((), ()))
    scores = []
    for k, mask in zip(keys, masks):
        s = lax.dot_general(k, qs, nt, preferred_element_type=F32) * SCALE
        scores.append(s if mask is None else jnp.where(mask, s, NEG_INF))
    m = sink_row
    for s in scores:
        m = jnp.maximum(m, jnp.max(s, axis=0, keepdims=True))
    den = jnp.exp(sink_row - m)
    out = None
    for s, v in zip(scores, vals):
        p = jnp.exp(s - m)
        den = den + jnp.sum(p, axis=0, keepdims=True)
        pv = lax.dot_general(v, p.astype(BF16), tn, preferred_element_type=F32)
        out = pv if out is None else out + pv
    return out / den


def _sink_row(sink_ref, layer, head0, queries_per_head):
    return jnp.concatenate(
        [jnp.full((1, queries_per_head), sink_ref[layer, head0 + g], F32) for g in range(GROUP)], axis=1)


def _attn_ctx_kernel(sink_ref, q_ref, k_ref, v_ref, o_ref, *, layer):
    kv = pl.program_id(1)
    t = q_ref.shape[0]
    qs = jnp.concatenate([q_ref[:, g * HEAD_DIM:(g + 1) * HEAD_DIM] for g in range(GROUP)], axis=0)
    out_t = _attend_t(qs, [k_ref[...].astype(BF16)], [v_ref[...].astype(BF16)], [None],
                      _sink_row(sink_ref, layer, kv * GROUP, t))
    for g in range(GROUP):
        o_ref[:, g * HEAD_DIM:(g + 1) * HEAD_DIM] = out_t[:, g * t:(g + 1) * t].T.astype(o_ref.dtype)


def _attn_ctx(u, q_col0, ukv, sink, layer, n_seq, t, n_kv):
    qw = GROUP * HEAD_DIM
    return pl.pallas_call(
        functools.partial(_attn_ctx_kernel, layer=layer),
        grid=(n_seq, n_kv),
        in_specs=[
            pl.BlockSpec(memory_space=pltpu.SMEM),
            pl.BlockSpec((t, qw), lambda b, kv: (b, q_col0 + kv)),
            pl.BlockSpec((t, HEAD_DIM), lambda b, kv: (b, kv)),
            pl.BlockSpec((t, HEAD_DIM), lambda b, kv: (b, n_kv + kv)),
        ],
        out_specs=pl.BlockSpec((t, qw), lambda b, kv: (b, kv)),
        out_shape=jax.ShapeDtypeStruct((n_seq * t, n_kv * qw), BF16),
        compiler_params=_params("parallel", "parallel"),
        name="attention_context",
    )(sink, u, ukv, ukv)


def _rope(x, cs):
    cos, sin = cs[:, :HEAD_DIM], cs[:, HEAD_DIM:]
    lane = lax.broadcasted_iota(jnp.int32, x.shape, 1)
    quarter = HEAD_DIM // 4
    partner = jnp.where((lane & quarter) == 0,
                        pltpu.roll(x, HEAD_DIM - quarter, axis=1), pltpu.roll(x, quarter, axis=1))
    return x * cos + partner * sin


def _attn_lat_kernel(sink_ref, q_ref, kvp_ref, kvc_ref, kvn_ref, csp_ref, csc_ref, csn_ref, ck_ref, cv_ref,
                     o_ref, *, layer, n_kv):
    i = pl.program_id(1)
    nb = pl.num_programs(1)
    kw = n_kv * HEAD_DIM
    rq = GROUP * BLK
    j = lax.broadcasted_iota(jnp.int32, (3 * BLK, rq), 0)
    r = lax.broadcasted_iota(jnp.int32, (3 * BLK, rq), 1) % BLK
    lo = jnp.where(i == 0, BLK, 0)
    hi = jnp.where(i == nb - 1, 2 * BLK, 3 * BLK)
    mask = (j >= r + (BLK - WINDOW)) & (j <= r + (BLK + WINDOW)) & (j >= lo) & (j < hi)
    cs_q = csc_ref[...]
    for kv in range(n_kv):
        hs = slice(kv * HEAD_DIM, (kv + 1) * HEAD_DIM)
        vs = slice(kw + kv * HEAD_DIM, kw + (kv + 1) * HEAD_DIM)
        keys = jnp.concatenate(
            [_rope(ref[:, hs], cs[...]).astype(BF16)
             for ref, cs in ((kvp_ref, csp_ref), (kvc_ref, csc_ref), (kvn_ref, csn_ref))], axis=0)
        vals = jnp.concatenate([ref[:, vs].astype(BF16) for ref in (kvp_ref, kvc_ref, kvn_ref)], axis=0)
        qs = jnp.concatenate(
            [_rope(q_ref[:, (kv * GROUP + g) * HEAD_DIM:(kv * GROUP + g + 1) * HEAD_DIM].astype(F32),
                   cs_q).astype(BF16) for g in range(GROUP)], axis=0)
        out_t = _attend_t(qs, [ck_ref[:, hs].astype(BF16), keys], [cv_ref[:, hs].astype(BF16), vals],
                          [None, mask], _sink_row(sink_ref, layer, kv * GROUP, BLK))
        for g in range(GROUP):
            h = kv * GROUP + g
            o_ref[:, h * HEAD_DIM:(h + 1) * HEAD_DIM] = out_t[:, g * BLK:(g + 1) * BLK].T.astype(o_ref.dtype)


def _attn_lat(u, q_col0, ukv, cs, cache_k, cache_v, sink, layer, row0, n_seq, t, n_kv):
    qw = n_kv * GROUP * HEAD_DIM
    kw = n_kv * HEAD_DIM
    nb = t // BLK
    blk0 = row0 // BLK
    past = cache_k.shape[2]
    kv_spec = lambda f: pl.BlockSpec((BLK, 2 * kw), lambda b, i: (blk0 + b * nb + f(i), 0))
    cs_spec = lambda f: pl.BlockSpec((BLK, 2 * HEAD_DIM), lambda b, i: (f(i), 0))
    prev = lambda i: jnp.maximum(i - 1, 0)
    cur = lambda i: i
    nxt = lambda i: jnp.minimum(i + 1, nb - 1)
    cache_spec = pl.BlockSpec((None, None, past, kw), lambda b, i: (b, layer, 0, 0))
    return pl.pallas_call(
        functools.partial(_attn_lat_kernel, layer=layer, n_kv=n_kv),
        grid=(n_seq, nb),
        in_specs=[
            pl.BlockSpec(memory_space=pltpu.SMEM),
            pl.BlockSpec((BLK, qw), lambda b, i: (blk0 + b * nb + i, q_col0)),
            kv_spec(prev), kv_spec(cur), kv_spec(nxt),
            cs_spec(prev), cs_spec(cur), cs_spec(nxt),
            cache_spec, cache_spec,
        ],
        out_specs=pl.BlockSpec((BLK, qw), lambda b, i: (b * nb + i, 0)),
        out_shape=jax.ShapeDtypeStruct((n_seq * t, qw), BF16),
        compiler_params=_params("parallel", "parallel"),
        name="attention_latent",
    )(sink, u, ukv, ukv, ukv, cs, cs, cs, cache_k, cache_v)


def _rope_table(t):
    rot_axis = HEAD_DIM // 2
    pos = jnp.arange(t)
    row = (pos // GRID_W).astype(F32)
    col = (pos % GRID_W).astype(F32)
    inv = ROPE_BASE ** (-jnp.arange(0, rot_axis, 2, dtype=F32) / rot_axis)
    ar, ac = row[:, None] * inv, col[:, None] * inv
    cos = jnp.concatenate([jnp.cos(ar), jnp.cos(ar), jnp.cos(ac), jnp.cos(ac)], axis=-1)
    sin = jnp.concatenate([-jnp.sin(ar), jnp.sin(ar), -jnp.sin(ac), jnp.sin(ac)], axis=-1)
    return jnp.concatenate([cos, sin], axis=-1)


def kernel(x_prompt, x_sample, cache_k, cache_v, c, c_ctx, norm1_g, norm2_g, w_ada, b_ada, w_in, w_fourier, w_dw, b_dw, conv_ln_g, conv_ln_b, w_pw, sink, w_out, w_mlp1, b_mlp1, w_mlp2, b_mlp2, final_g):
    batch, seq, d = x_prompt.shape
    dec_batch, dec_seq, _ = x_sample.shape
    depth = w_in.shape[0]
    n_heads = sink.shape[1]
    n_kv = n_heads // GROUP
    q_w, kv_w = n_heads * HEAD_DIM, n_kv * HEAD_DIM
    f_w = w_fourier.shape[1] * w_fourier.shape[2]
    conv_ch = w_pw.shape[1]
    in_cols = w_in.shape[2]
    d_ff = w_mlp1.shape[2]
    assert in_cols == f_w + 2 * conv_ch + q_w + 2 * kv_w and w_fourier.shape[1] == F_GROUPS
    assert f_w == conv_ch == GROUP * HEAD_DIM * 2 and q_w == 2 * f_w, "column blocks below assume these widths"
    assert w_dw.shape[1] == CONV_K and dec_seq % GRID_W == 0

    n_ctx, n_lat = batch * seq, dec_batch * dec_seq
    m = n_ctx + n_lat
    rows = _Rows(n_ctx, dec_seq)
    assert n_ctx % dec_seq == 0, "latent sequences must start on a multiple of their length"
    tm = _pick(math.gcd(n_ctx, dec_seq), (1024, 512, 256))
    tn = 1024
    main_cols = q_w + f_w + 2 * conv_ch
    q_split = f_w + 2 * conv_ch
    blk_fourier, blk_glu = q_w // f_w, q_w // f_w + 1

    cond_rows = -(-(1 + dec_batch) // 8) * 8
    cond = jnp.zeros((cond_rows, d), F32).at[0].set(c_ctx).at[1:1 + dec_batch].set(c)
    mod = _ada(cond, w_ada, b_ada).reshape(depth, cond_rows, 6, 1, d)
    shift_a, scale_a, gate_a, shift_m, scale_m, gate_m = range(6)

    w_in_main = jnp.concatenate(
        [w_in[:, :, q_split:q_split + q_w].astype(BF16), w_in[:, :, :q_split].astype(BF16)], axis=2)
    w_in_kv = w_in[:, :, q_split + q_w:].astype(BF16)
    w_out_b = w_out.astype(BF16)
    w1_b, w2_b, w_pw_b = w_mlp1.astype(BF16), w_mlp2.astype(BF16), w_pw.astype(BF16)
    f_gc = f_w // F_GROUPS
    wf_b = w_fourier.astype(BF16)
    gi = jnp.arange(f_gc, dtype=jnp.int32)
    ang_ch = ((gi[:, None] * gi[None, :]) % f_gc).astype(F32) * (2.0 * math.pi / f_gc)
    chan_cs = (jnp.concatenate([jnp.cos(ang_ch), jnp.sin(ang_ch)], axis=1) * f_gc ** -0.5).astype(BF16)
    dft_ctx = _dft_tables(seq)
    dft_lat = _dft_tables(dec_seq)
    cs_lat = _rope_table(dec_seq)
    cache_k2 = cache_k.reshape(dec_batch, depth, cache_k.shape[2], kv_w)
    cache_v2 = cache_v.reshape(dec_batch, depth, cache_v.shape[2], kv_w)

    x = jnp.concatenate([x_prompt.reshape(n_ctx, d), x_sample.reshape(n_lat, d)], axis=0)
    gate_spec = lambda layer, which: pl.BlockSpec(
        (None, None, None, 1, tn), lambda i, j, *_: (layer, rows.mod_row(i, tm), which, 0, j))
    w_tile = lambda layer, k_rows, k_blk: pl.BlockSpec((None, k_rows, tn), lambda i, j: (layer, k_blk, j))
    b1 = b_mlp1.reshape(depth, 1, d_ff)
    b2 = b_mlp2.reshape(depth, 1, d)

    new_k, new_v = [], []
    for l in range(depth):
        n1 = _norm_mod(x, norm1_g, mod, l, scale_a, shift_a, rows)
        u = _mm([n1], [(d, 0)], [w_in_main], [w_tile(l, d, 0)], main_cols, BF16, tm=tm, tn=tn,
                name="in_proj_main")
        ukv = _mm([n1], [(d, 0)], [w_in_kv], [w_tile(l, d, 0)], 2 * kv_w, F32, tm=tm, tn=tn,
                  name="in_proj_kv")
        new_k.append(ukv[:n_ctx, :kv_w].reshape(batch, seq, n_kv, HEAD_DIM))
        new_v.append(ukv[:n_ctx, kv_w:].reshape(batch, seq, n_kv, HEAD_DIM))

        xcs = _fourier_chan(u, blk_fourier, chan_cs, f_w, tm)
        yf = jnp.concatenate([
            _fourier_pos(xcs, dft_ctx, wf_b, l, 0, batch, seq).reshape(n_ctx, f_w),
            _fourier_pos(xcs, dft_lat, wf_b, l, n_ctx, dec_batch, dec_seq).reshape(n_lat, f_w)], axis=0)

        yc = _conv(u, blk_glu, w_dw, b_dw, conv_ln_g, conv_ln_b, w_pw_b, l, rows, seq)

        att = jnp.concatenate([
            _attn_ctx(u, 0, ukv, sink, l, batch, seq, n_kv),
            _attn_lat(u, 0, ukv, cs_lat, cache_k2, cache_v2, sink, l, n_ctx, dec_batch, dec_seq, n_kv)],
            axis=0)

        x = _mm([yf, yc, att], [(f_w, 0), (conv_ch, 0), (q_w, 0)], [w_out_b, w_out_b, w_out_b],
                [w_tile(l, f_w, 0), w_tile(l, conv_ch, 1), w_tile(l, q_w, 1)],
                d, F32, tm=tm, tn=tn, res=(x, mod, gate_spec(l, gate_a)), name="out_proj")

        n2 = _norm_mod(x, norm2_g, mod, l, scale_m, shift_m, rows)
        h = _mm([n2], [(d, 0)], [w1_b], [w_tile(l, d, 0)], d_ff, BF16, tm=tm, tn=tn,
                bias=(b1, pl.BlockSpec((None, 1, tn), lambda i, j, l=l: (l, 0, j))),
                relu2=True, name="mlp_up")
        x = _mm_kgrid_res(h, w2_b, l, b2, x, mod, gate_spec(l, gate_m),
                          tm=tm, tn=tn, tk=min(2048, d_ff), name="mlp_down")

    y_prompt = _final_norm(x, final_g, 0, n_ctx).reshape(batch, seq, d)
    y_sample = _final_norm(x, final_g, n_ctx, n_lat).reshape(dec_batch, dec_seq, d)
    return (y_prompt, y_sample, jnp.stack(new_k, axis=1), jnp.stack(new_v, axis=1))
```

```python
import functools
import math

import jax
import jax.numpy as jnp
from jax import lax
from jax.experimental import pallas as pl
from jax.experimental.pallas import tpu as pltpu

GRID_W = 64
HEAD_DIM = 128
GROUP = 4
F_GROUPS = 8
CONV_K = 31
CONV_HALO = 16
SUBLANES = 8
DFT_EXTRA_ROWS = 16
WINDOW = 128
BLK = 128
ROPE_BASE = 10000.0
EPS = 1e-6
NEG_INF = -1e30
SCALE = HEAD_DIM ** -0.5
LOG2_E = math.log2(math.e)

V7X_VMEM_BYTES = 64 * 1024 * 1024
VMEM_LIMIT_BYTES = V7X_VMEM_BYTES - 8 * 1024 * 1024

F32 = jnp.float32
BF16 = jnp.bfloat16


def _params(*semantics):
    return pltpu.CompilerParams(dimension_semantics=semantics, vmem_limit_bytes=VMEM_LIMIT_BYTES)


def _pick(n, candidates):
    for c in candidates:
        if n % c == 0:
            return c
    raise ValueError(f"no tile in {candidates} divides {n}")


class _Rows:
    def __init__(self, n_ctx, t_lat):
        self.n_ctx, self.t_lat = n_ctx, t_lat

    def mod_row(self, i, tm):
        n_ctx_tiles = self.n_ctx // tm
        per_seq = self.t_lat // tm
        return jnp.where(i < n_ctx_tiles, 0, 1 + jnp.maximum(i - n_ctx_tiles, 0) // per_seq)


def _split_bf16(v):
    hi = v.astype(BF16)
    return hi, (v - hi.astype(F32)).astype(BF16)


def _ada_kernel(c_ref, w_ref, b_ref, o_ref, *, k_chunk):
    acc = b_ref[...] + jnp.zeros(o_ref.shape, F32)
    for k0 in range(0, c_ref.shape[1], k_chunk):
        c = c_ref[:, k0:k0 + k_chunk]
        s_hi, s_lo = _split_bf16(c * jax.nn.sigmoid(c))
        w_hi, w_lo = _split_bf16(w_ref[k0:k0 + k_chunk, :])
        acc = acc + jnp.dot(s_hi, w_hi, preferred_element_type=F32)
        acc = acc + jnp.dot(s_lo, w_hi, preferred_element_type=F32)
        acc = acc + jnp.dot(s_hi, w_lo, preferred_element_type=F32)
    o_ref[...] = acc


def _ada(cond, w_ada, b_ada):
    depth, d, n = w_ada.shape
    rows = cond.shape[0]
    tn = _pick(n, (512, 256, 128))
    return pl.pallas_call(
        functools.partial(_ada_kernel, k_chunk=_pick(d, (512, 256, 128))),
        grid=(depth, n // tn),
        in_specs=[
            pl.BlockSpec((rows, d), lambda l, j: (0, 0)),
            pl.BlockSpec((None, d, tn), lambda l, j: (l, 0, j)),
            pl.BlockSpec((None, 1, tn), lambda l, j: (l, 0, j)),
        ],
        out_specs=pl.BlockSpec((None, rows, tn), lambda l, j: (l, 0, j)),
        out_shape=jax.ShapeDtypeStruct((depth, rows, n), F32),
        compiler_params=_params("parallel", "parallel"),
        name="ada_modulation",
    )(cond, w_ada, b_ada.reshape(depth, 1, n))


def _norm_mod_kernel(x_ref, g_ref, scale_ref, shift_ref, o_ref):
    x = x_ref[...]
    y = x * lax.rsqrt(jnp.mean(x * x, axis=-1, keepdims=True) + EPS) * g_ref[...]
    o_ref[...] = (y * (1.0 + scale_ref[...]) + shift_ref[...]).astype(o_ref.dtype)


def _norm_mod(x, g, mod, layer, which_scale, which_shift, rows):
    m, d = x.shape
    tm = _pick(math.gcd(rows.n_ctx, rows.t_lat), (512, 256, 128))
    mod_spec = lambda which: pl.BlockSpec(
        (None, None, None, 1, d), lambda i: (layer, rows.mod_row(i, tm), which, 0, 0))
    return pl.pallas_call(
        _norm_mod_kernel,
        grid=(m // tm,),
        in_specs=[
            pl.BlockSpec((tm, d), lambda i: (i, 0)),
            pl.BlockSpec((None, 1, d), lambda i: (layer, 0, 0)),
            mod_spec(which_scale),
            mod_spec(which_shift),
        ],
        out_specs=pl.BlockSpec((tm, d), lambda i: (i, 0)),
        out_shape=jax.ShapeDtypeStruct((m, d), BF16),
        compiler_params=_params("parallel"),
        name="rmsnorm_modulate",
    )(x, g.reshape(g.shape[0], 1, d), mod, mod)


def _final_norm_kernel(x_ref, g_ref, o_ref):
    x = x_ref[...]
    y = x * lax.rsqrt(jnp.mean(x * x, axis=-1, keepdims=True) + EPS)
    o_ref[...] = y * g_ref[...]


def _final_norm(x, g, row0, n_rows):
    d = x.shape[1]
    tm = _pick(math.gcd(row0, n_rows) if row0 else n_rows, (512, 256, 128))
    off = row0 // tm
    return pl.pallas_call(
        _final_norm_kernel,
        grid=(n_rows // tm,),
        in_specs=[
            pl.BlockSpec((tm, d), lambda i: (off + i, 0)),
            pl.BlockSpec((1, d), lambda i: (0, 0)),
        ],
        out_specs=pl.BlockSpec((tm, d), lambda i: (i, 0)),
        out_shape=jax.ShapeDtypeStruct((n_rows, d), F32),
        compiler_params=_params("parallel"),
        name="final_rmsnorm",
    )(x, g.reshape(1, d))


def _mm_kernel(*refs, n_pairs, has_bias, relu2, has_res):
    it = iter(refs)
    lhs = [next(it) for _ in range(n_pairs)]
    rhs = [next(it) for _ in range(n_pairs)]
    bias_ref = next(it) if has_bias else None
    x_ref = next(it) if has_res else None
    gate_ref = next(it) if has_res else None
    o_ref = next(it)
    acc = None
    for a_ref, b_ref in zip(lhs, rhs):
        part = jnp.dot(a_ref[...], b_ref[...], preferred_element_type=F32)
        acc = part if acc is None else acc + part
    if has_bias:
        acc = acc + bias_ref[...]
    if relu2:
        acc = jnp.square(jnp.maximum(acc, 0.0))
    if has_res:
        acc = x_ref[...] + gate_ref[...] * acc
    o_ref[...] = acc.astype(o_ref.dtype)


def _mm(lhs, lhs_cols, rhs, rhs_specs, n_out, out_dtype, *, tm, tn, bias=None, relu2=False, res=None, name):
    m = lhs[0].shape[0]
    in_specs = [pl.BlockSpec((tm, w), lambda i, j, blk=blk: (i, blk)) for w, blk in lhs_cols]
    in_specs += list(rhs_specs)
    args = list(lhs) + list(rhs)
    if bias is not None:
        args.append(bias[0])
        in_specs.append(bias[1])
    if res is not None:
        x, gate, gate_spec = res
        args += [x, gate]
        in_specs += [pl.BlockSpec((tm, tn), lambda i, j: (i, j)), gate_spec]
    return pl.pallas_call(
        functools.partial(_mm_kernel, n_pairs=len(lhs), has_bias=bias is not None, relu2=relu2,
                          has_res=res is not None),
        grid=(m // tm, n_out // tn),
        in_specs=in_specs,
        out_specs=pl.BlockSpec((tm, tn), lambda i, j: (i, j)),
        out_shape=jax.ShapeDtypeStruct((m, n_out), out_dtype),
        compiler_params=_params("parallel", "parallel"),
        name=name,
    )(*args)


def _mm_kgrid_kernel(a_ref, b_ref, bias_ref, x_ref, gate_ref, o_ref, acc_ref):
    k = pl.program_id(2)

    @pl.when(k == 0)
    def _():
        acc_ref[...] = jnp.zeros_like(acc_ref)

    acc_ref[...] += jnp.dot(a_ref[...], b_ref[...], preferred_element_type=F32)

    @pl.when(k == pl.num_programs(2) - 1)
    def _():
        o_ref[...] = x_ref[...] + gate_ref[...] * (acc_ref[...] + bias_ref[...])


def _mm_kgrid_res(a, w, layer, bias, x, gate, gate_spec, *, tm, tn, tk, name):
    m, k = a.shape
    n = w.shape[2]
    return pl.pallas_call(
        _mm_kgrid_kernel,
        grid=(m // tm, n // tn, k // tk),
        in_specs=[
            pl.BlockSpec((tm, tk), lambda i, j, kk: (i, kk)),
            pl.BlockSpec((None, tk, tn), lambda i, j, kk: (layer, kk, j)),
            pl.BlockSpec((None, 1, tn), lambda i, j, kk: (layer, 0, j)),
            pl.BlockSpec((tm, tn), lambda i, j, kk: (i, j)),
            gate_spec,
        ],
        out_specs=pl.BlockSpec((tm, tn), lambda i, j, kk: (i, j)),
        out_shape=jax.ShapeDtypeStruct((m, n), F32),
        scratch_shapes=[pltpu.VMEM((tm, tn), F32)],
        compiler_params=_params("parallel", "parallel", "arbitrary"),
        name=name,
    )(a, w, bias, x, gate)


def _fourier_chan_kernel(x_ref, cs_ref, o_ref):
    f_gc = cs_ref.shape[0]
    f_w = x_ref.shape[1]
    for g in range(f_w // f_gc):
        cols = slice(g * f_gc, (g + 1) * f_gc)
        r = jnp.dot(x_ref[:, cols], cs_ref[...], preferred_element_type=F32)
        o_ref[:, cols] = r[:, :f_gc].astype(o_ref.dtype)
        o_ref[:, f_w + g * f_gc:f_w + (g + 1) * f_gc] = r[:, f_gc:].astype(o_ref.dtype)


def _fourier_chan(u, col_blk, chan_cs, f_w, tm):
    m = u.shape[0]
    return pl.pallas_call(
        _fourier_chan_kernel,
        grid=(m // tm,),
        in_specs=[
            pl.BlockSpec((tm, f_w), lambda i: (i, col_blk)),
            pl.BlockSpec(chan_cs.shape, lambda i: (0, 0)),
        ],
        out_specs=pl.BlockSpec((tm, 2 * f_w), lambda i: (i, 0)),
        out_shape=jax.ShapeDtypeStruct((m, 2 * f_w), BF16),
        compiler_params=_params("parallel"),
        name="fourier_channel",
    )(u, chan_cs)


def _fourier_pos_kernel(ct_ref, ctx_ref, st_ref, stx_ref, xc_ref, xs_ref, wf_ref, jrev_ref, lo_ref, hi_ref, d_ref):
    tq = ct_ref.shape[0]
    c_rows = jnp.concatenate([ct_ref[...], ctx_ref[...]], axis=0)
    s_rows = jnp.concatenate([st_ref[...], stx_ref[...]], axis=0)
    p = jnp.dot(c_rows, xc_ref[...], preferred_element_type=F32)
    q_neg = jnp.dot(s_rows, xs_ref[...], preferred_element_type=F32)
    d_ref[...] = p - q_neg
    lo = (p[:tq] + q_neg[:tq]).astype(BF16)
    up = d_ref[1:tq + 1, :].astype(BF16)
    f_gc = wf_ref.shape[1]
    z_up = []
    for g in range(wf_ref.shape[0]):
        cols = slice(g * f_gc, (g + 1) * f_gc)
        lo_ref[:, cols] = jnp.dot(lo[:, cols], wf_ref[g], preferred_element_type=F32).astype(lo_ref.dtype)
        z_up.append(jnp.dot(up[:, cols], wf_ref[g], preferred_element_type=F32).astype(BF16))
    hi_ref[...] = jnp.dot(jrev_ref[...], jnp.concatenate(z_up, axis=1),
                          preferred_element_type=F32).astype(hi_ref.dtype)


def _fourier_pos(xcs, dft, wf, layer, row0, n_seq, t):
    ct, st_neg, jrev = dft
    f_w = xcs.shape[1] // 2
    tq = jrev.shape[0]
    nj = t // (2 * tq)
    seq0 = row0 // t
    xb = DFT_EXTRA_ROWS
    main = lambda: pl.BlockSpec((tq, t), lambda b, j: (j, 0))
    extra = lambda: pl.BlockSpec((xb, t), lambda b, j: ((j + 1) * (tq // xb), 0))
    half = jax.ShapeDtypeStruct((n_seq * (t // 2), f_w), BF16)
    lo, hi = pl.pallas_call(
        _fourier_pos_kernel,
        grid=(n_seq, nj),
        in_specs=[
            main(), extra(), main(), extra(),
            pl.BlockSpec((t, f_w), lambda b, j: (seq0 + b, 0)),
            pl.BlockSpec((t, f_w), lambda b, j: (seq0 + b, 1)),
            pl.BlockSpec((None,) + wf.shape[1:], lambda b, j: (layer, 0, 0, 0)),
            pl.BlockSpec((tq, tq), lambda b, j: (0, 0)),
        ],
        out_specs=[pl.BlockSpec((tq, f_w), lambda b, j: (b * nj + j, 0)),
                   pl.BlockSpec((tq, f_w), lambda b, j: (b * nj + nj - 1 - j, 0))],
        out_shape=[half, half],
        scratch_shapes=[pltpu.VMEM((tq + xb, f_w), F32)],
        compiler_params=_params("parallel", "parallel"),
        name=f"fourier_position_t{t}",
    )(ct, ct, st_neg, st_neg, xcs, xcs, wf, jrev)
    return jnp.stack([lo.reshape(n_seq, t // 2, f_w), hi.reshape(n_seq, t // 2, f_w)], axis=1)


def _dft_tables(t):
    tq = _pick(t // 2, (256, 128))
    rows = jnp.arange(t // 2 + DFT_EXTRA_ROWS, dtype=jnp.int32)
    cols = jnp.arange(t, dtype=jnp.int32)
    ang = ((rows[:, None] * cols[None, :]) % t).astype(F32) * (2.0 * math.pi / t)
    s = t ** -0.5
    jrev = jnp.eye(tq, dtype=BF16)[::-1]
    return (jnp.cos(ang) * s).astype(BF16), (-jnp.sin(ang) * s).astype(BF16), jrev


def _conv_kernel(a_ref, g_ref, ap_ref, gp_ref, an_ref, gn_ref, wdw_ref, bdw_ref, lng_ref, lnb_ref,
                 wpw_ref, o_ref, hext_ref, shift_ref, conv_ref, *, rows, tt, row_chunk, lane_chunk):
    i = pl.program_id(0)
    n_ctx_tiles = rows.n_ctx // tt
    per_seq = rows.t_lat // tt
    pos = jnp.where(i < n_ctx_tiles, 0, jnp.maximum(i - n_ctx_tiles, 0) % per_seq)
    last = jnp.where(i < n_ctx_tiles, 0, per_seq - 1)

    def glu(a, g):
        return a[...].astype(F32) * jax.nn.sigmoid(g[...].astype(F32))

    halo = CONV_HALO
    hext_ref[0:halo, :] = jnp.where(pos > 0, glu(ap_ref, gp_ref), 0.0)
    hext_ref[halo:halo + tt, :] = glu(a_ref, g_ref)
    hext_ref[halo + tt:2 * halo + tt, :] = jnp.where(pos < last, glu(an_ref, gn_ref), 0.0)

    n_shift_rows = shift_ref.shape[1]
    for s in range(SUBLANES):
        shift_ref[s] = hext_ref[s:s + n_shift_rows, :]

    c = conv_ref.shape[1]
    first_tap = halo - CONV_K // 2

    groups = row_chunk // SUBLANES

    def chunk(r, carry):
        r0 = pl.multiple_of(r * row_chunk, row_chunk)
        for l0 in range(0, c, lane_chunk):
            lanes = slice(l0, l0 + lane_chunk)
            accs = [bdw_ref[:, lanes]] * groups
            for s in range(SUBLANES):
                qs = [q for q in range(-(-CONV_K // SUBLANES) + 1) if 0 <= q * SUBLANES + s - first_tap < CONV_K]
                taps = {q: wdw_ref[q * SUBLANES + s - first_tap, :, lanes] for q in qs}
                for u in range(min(qs), max(qs) + groups):
                    start = pl.multiple_of(r0 + u * SUBLANES, SUBLANES)
                    h = shift_ref[s, pl.ds(start, SUBLANES), lanes]
                    for q in qs:
                        if 0 <= u - q < groups:
                            accs[u - q] = accs[u - q] + taps[q] * h
            for a, acc in enumerate(accs):
                conv_ref[pl.ds(pl.multiple_of(r0 + a * SUBLANES, SUBLANES), SUBLANES), lanes] = acc
        return carry

    lax.fori_loop(0, tt // row_chunk, chunk, 0)

    h = conv_ref[...]
    mu = jnp.mean(h, axis=-1, keepdims=True)
    var = jnp.mean(jnp.square(h - mu), axis=-1, keepdims=True)
    y = (h - mu) * lax.rsqrt(var + EPS) * lng_ref[...] + lnb_ref[...]
    y = y * jax.nn.sigmoid(y)
    o_ref[...] = jnp.dot(y.astype(BF16), wpw_ref[...], preferred_element_type=F32).astype(o_ref.dtype)


def _conv(u, col_a, w_dw, b_dw, ln_g, ln_b, w_pw, layer, rows, seq_ctx):
    m = u.shape[0]
    c = w_pw.shape[1]
    tt = seq_ctx
    assert rows.t_lat % tt == 0 and rows.n_ctx % tt == 0 and tt % CONV_HALO == 0
    hb = tt // CONV_HALO
    n_hblocks = m // CONV_HALO
    cur = lambda col: pl.BlockSpec((tt, c), lambda i: (i, col))
    prev = lambda col: pl.BlockSpec((CONV_HALO, c), lambda i: (jnp.maximum(i * hb - 1, 0), col))
    nxt = lambda col: pl.BlockSpec((CONV_HALO, c), lambda i: (jnp.minimum((i + 1) * hb, n_hblocks - 1), col))
    vec = lambda: pl.BlockSpec((None, 1, c), lambda i: (layer, 0, 0))
    depth, taps = w_dw.shape[:2]
    w_dw_rep = jnp.broadcast_to(w_dw[:, :, None, :], (depth, taps, SUBLANES, c))
    b_dw_rep = jnp.broadcast_to(b_dw[:, None, :], (depth, SUBLANES, c))
    return pl.pallas_call(
        functools.partial(_conv_kernel, rows=rows, tt=tt, row_chunk=32, lane_chunk=512),
        grid=(m // tt,),
        in_specs=[
            cur(col_a), cur(col_a + 1), prev(col_a), prev(col_a + 1), nxt(col_a), nxt(col_a + 1),
            pl.BlockSpec((None, taps, SUBLANES, c), lambda i: (layer, 0, 0, 0)),
            pl.BlockSpec((None, SUBLANES, c), lambda i: (layer, 0, 0)),
            vec(), vec(),
            pl.BlockSpec((None, c, c), lambda i: (layer, 0, 0)),
        ],
        out_specs=pl.BlockSpec((tt, c), lambda i: (i, 0)),
        out_shape=jax.ShapeDtypeStruct((m, c), BF16),
        scratch_shapes=[pltpu.VMEM((tt + 2 * CONV_HALO, c), F32),
                        pltpu.VMEM((SUBLANES, tt + 2 * CONV_HALO - SUBLANES, c), F32),
                        pltpu.VMEM((tt, c), F32)],
        compiler_params=_params("parallel"),
        name="conformer_conv",
    )(u, u, u, u, u, u, w_dw_rep, b_dw_rep, ln_g.reshape(-1, 1, c), ln_b.reshape(-1, 1, c), w_pw)


def _attend_t(qs, keys, vals, masks, sink_row):
    nt = (((1,), (1,)), ((), ()))
    tn = (((0,), (0,)), ((), ()))
    sink_row = sink_row * LOG2_E
    scores = []
    for k, mask in zip(keys, masks):
        s = lax.dot_general(k, qs, nt, preferred_element_type=F32) * (SCALE * LOG2_E)
        scores.append(s if mask is None else mask(s))
    m = sink_row
    for s in scores:
        m = jnp.maximum(m, jnp.max(s, axis=0, keepdims=True))
    den = jnp.exp2(sink_row - m)
    out = None
    for s, v in zip(scores, vals):
        p = jnp.exp2(s - m)
        den = den + jnp.sum(p, axis=0, keepdims=True)
        pv = lax.dot_general(v, p.astype(BF16), tn, preferred_element_type=F32)
        out = pv if out is None else out + pv
    return out / den


def _sink_row(sink_ref, layer, head0, queries_per_head):
    return jnp.concatenate(
        [jnp.full((1, queries_per_head), sink_ref[layer, head0 + g], F32) for g in range(GROUP)], axis=1)


def _attn_ctx_kernel(sink_ref, q_ref, k_ref, v_ref, o_ref, *, layer):
    kv = pl.program_id(1)
    t = q_ref.shape[0]
    qs = jnp.concatenate([q_ref[:, g * HEAD_DIM:(g + 1) * HEAD_DIM] for g in range(GROUP)], axis=0)
    out_t = _attend_t(qs, [k_ref[...].astype(BF16)], [v_ref[...].astype(BF16)], [None],
                      _sink_row(sink_ref, layer, kv * GROUP, t))
    for g in range(GROUP):
        o_ref[:, g * HEAD_DIM:(g + 1) * HEAD_DIM] = out_t[:, g * t:(g + 1) * t].T.astype(o_ref.dtype)


def _attn_ctx(u, q_col0, ukv, sink, layer, n_seq, t, n_kv):
    qw = GROUP * HEAD_DIM
    return pl.pallas_call(
        functools.partial(_attn_ctx_kernel, layer=layer),
        grid=(n_seq, n_kv),
        in_specs=[
            pl.BlockSpec(memory_space=pltpu.SMEM),
            pl.BlockSpec((t, qw), lambda b, kv: (b, q_col0 + kv)),
            pl.BlockSpec((t, HEAD_DIM), lambda b, kv: (b, kv)),
            pl.BlockSpec((t, HEAD_DIM), lambda b, kv: (b, n_kv + kv)),
        ],
        out_specs=pl.BlockSpec((t, qw), lambda b, kv: (b, kv)),
        out_shape=jax.ShapeDtypeStruct((n_seq * t, n_kv * qw), BF16),
        compiler_params=_params("parallel", "parallel"),
        name="attention_context",
    )(sink, u, ukv, ukv)


def _rope(x, cs):
    cos, sin = cs[:, :HEAD_DIM], cs[:, HEAD_DIM:]
    lane = lax.broadcasted_iota(jnp.int32, x.shape, 1)
    quarter = HEAD_DIM // 4
    partner = jnp.where((lane & quarter) == 0,
                        pltpu.roll(x, HEAD_DIM - quarter, axis=1), pltpu.roll(x, quarter, axis=1))
    return x * cos + partner * sin


def _attn_lat_kernel(sink_ref, q_ref, kvp_ref, kvc_ref, kvn_ref, csp_ref, csc_ref, csn_ref, ck_ref, cv_ref,
                     o_ref, *, layer, n_kv):
    i = pl.program_id(1)
    nb = pl.num_programs(1)
    kw = n_kv * HEAD_DIM
    rq = GROUP * BLK
    j = lax.broadcasted_iota(jnp.int32, (BLK, rq), 0)
    r = lax.broadcasted_iota(jnp.int32, (BLK, rq), 1) % BLK
    see_prev = (j >= r) & (i > 0)
    see_next = (j <= r) & (i < nb - 1)

    def window_mask(s):
        return jnp.concatenate([jnp.where(see_prev, s[:BLK], NEG_INF), s[BLK:2 * BLK],
                                jnp.where(see_next, s[2 * BLK:], NEG_INF)], axis=0)

    cs_q = csc_ref[...]
    for kv in range(n_kv):
        hs = slice(kv * HEAD_DIM, (kv + 1) * HEAD_DIM)
        vs = slice(kw + kv * HEAD_DIM, kw + (kv + 1) * HEAD_DIM)
        keys = jnp.concatenate(
            [_rope(ref[:, hs], cs[...]).astype(BF16)
             for ref, cs in ((kvp_ref, csp_ref), (kvc_ref, csc_ref), (kvn_ref, csn_ref))], axis=0)
        vals = jnp.concatenate([ref[:, vs].astype(BF16) for ref in (kvp_ref, kvc_ref, kvn_ref)], axis=0)
        qs = jnp.concatenate(
            [_rope(q_ref[:, (kv * GROUP + g) * HEAD_DIM:(kv * GROUP + g + 1) * HEAD_DIM].astype(F32),
                   cs_q).astype(BF16) for g in range(GROUP)], axis=0)
        out_t = _attend_t(qs, [ck_ref[:, hs].astype(BF16), keys], [cv_ref[:, hs].astype(BF16), vals],
                          [None, window_mask], _sink_row(sink_ref, layer, kv * GROUP, BLK))
        for g in range(GROUP):
            h = kv * GROUP + g
            o_ref[:, h * HEAD_DIM:(h + 1) * HEAD_DIM] = out_t[:, g * BLK:(g + 1) * BLK].T.astype(o_ref.dtype)


def _attn_lat(u, q_col0, ukv, cs, cache_k, cache_v, sink, layer, row0, n_seq, t, n_kv):
    qw = n_kv * GROUP * HEAD_DIM
    kw = n_kv * HEAD_DIM
    nb = t // BLK
    blk0 = row0 // BLK
    past = cache_k.shape[2]
    kv_spec = lambda f: pl.BlockSpec((BLK, 2 * kw), lambda b, i: (blk0 + b * nb + f(i), 0))
    cs_spec = lambda f: pl.BlockSpec((BLK, 2 * HEAD_DIM), lambda b, i: (f(i), 0))
    prev = lambda i: jnp.maximum(i - 1, 0)
    cur = lambda i: i
    nxt = lambda i: jnp.minimum(i + 1, nb - 1)
    cache_spec = pl.BlockSpec((None, None, past, kw), lambda b, i: (b, layer, 0, 0))
    return pl.pallas_call(
        functools.partial(_attn_lat_kernel, layer=layer, n_kv=n_kv),
        grid=(n_seq, nb),
        in_specs=[
            pl.BlockSpec(memory_space=pltpu.SMEM),
            pl.BlockSpec((BLK, qw), lambda b, i: (blk0 + b * nb + i, q_col0)),
            kv_spec(prev), kv_spec(cur), kv_spec(nxt),
            cs_spec(prev), cs_spec(cur), cs_spec(nxt),
            cache_spec, cache_spec,
        ],
        out_specs=pl.BlockSpec((BLK, qw), lambda b, i: (b * nb + i, 0)),
        out_shape=jax.ShapeDtypeStruct((n_seq * t, qw), BF16),
        compiler_params=_params("parallel", "parallel"),
        name="attention_latent",
    )(sink, u, ukv, ukv, ukv, cs, cs, cs, cache_k, cache_v)


def _rope_table(t):
    rot_axis = HEAD_DIM // 2
    pos = jnp.arange(t)
    row = (pos // GRID_W).astype(F32)
    col = (pos % GRID_W).astype(F32)
    inv = ROPE_BASE ** (-jnp.arange(0, rot_axis, 2, dtype=F32) / rot_axis)
    ar, ac = row[:, None] * inv, col[:, None] * inv
    cos = jnp.concatenate([jnp.cos(ar), jnp.cos(ar), jnp.cos(ac), jnp.cos(ac)], axis=-1)
    sin = jnp.concatenate([-jnp.sin(ar), jnp.sin(ar), -jnp.sin(ac), jnp.sin(ac)], axis=-1)
    return jnp.concatenate([cos, sin], axis=-1)


def kernel(x_prompt, x_sample, cache_k, cache_v, c, c_ctx, norm1_g, norm2_g, w_ada, b_ada, w_in, w_fourier, w_dw, b_dw, conv_ln_g, conv_ln_b, w_pw, sink, w_out, w_mlp1, b_mlp1, w_mlp2, b_mlp2, final_g):
    batch, seq, d = x_prompt.shape
    dec_batch, dec_seq, _ = x_sample.shape
    depth = w_in.shape[0]
    n_heads = sink.shape[1]
    n_kv = n_heads // GROUP
    q_w, kv_w = n_heads * HEAD_DIM, n_kv * HEAD_DIM
    f_w = w_fourier.shape[1] * w_fourier.shape[2]
    conv_ch = w_pw.shape[1]
    in_cols = w_in.shape[2]
    d_ff = w_mlp1.shape[2]
    assert in_cols == f_w + 2 * conv_ch + q_w + 2 * kv_w and w_fourier.shape[1] == F_GROUPS
    assert f_w == conv_ch == GROUP * HEAD_DIM * 2 and q_w == 2 * f_w, "column blocks below assume these widths"
    assert w_dw.shape[1] == CONV_K and dec_seq % GRID_W == 0 and WINDOW == BLK

    n_ctx, n_lat = batch * seq, dec_batch * dec_seq
    m = n_ctx + n_lat
    rows = _Rows(n_ctx, dec_seq)
    assert n_ctx % dec_seq == 0, "latent sequences must start on a multiple of their length"
    tm = _pick(math.gcd(n_ctx, dec_seq), (1024, 512, 256))
    tn = 1024
    main_cols = q_w + f_w + 2 * conv_ch
    q_split = f_w + 2 * conv_ch
    blk_fourier, blk_glu = q_w // f_w, q_w // f_w + 1

    cond_rows = -(-(1 + dec_batch) // 8) * 8
    cond = jnp.zeros((cond_rows, d), F32).at[0].set(c_ctx).at[1:1 + dec_batch].set(c)
    mod = _ada(cond, w_ada, b_ada).reshape(depth, cond_rows, 6, 1, d)
    shift_a, scale_a, gate_a, shift_m, scale_m, gate_m = range(6)

    w_in_main = jnp.concatenate(
        [w_in[:, :, q_split:q_split + q_w].astype(BF16), w_in[:, :, :q_split].astype(BF16)], axis=2)
    w_in_kv = w_in[:, :, q_split + q_w:].astype(BF16)
    w_out_b = w_out.astype(BF16)
    w1_b, w2_b, w_pw_b = w_mlp1.astype(BF16), w_mlp2.astype(BF16), w_pw.astype(BF16)
    f_gc = f_w // F_GROUPS
    wf_b = w_fourier.astype(BF16)
    gi = jnp.arange(f_gc, dtype=jnp.int32)
    ang_ch = ((gi[:, None] * gi[None, :]) % f_gc).astype(F32) * (2.0 * math.pi / f_gc)
    chan_cs = (jnp.concatenate([jnp.cos(ang_ch), jnp.sin(ang_ch)], axis=1) * f_gc ** -0.5).astype(BF16)
    dft_ctx = _dft_tables(seq)
    dft_lat = _dft_tables(dec_seq)
    cs_lat = _rope_table(dec_seq)
    cache_k2 = cache_k.reshape(dec_batch, depth, cache_k.shape[2], kv_w)
    cache_v2 = cache_v.reshape(dec_batch, depth, cache_v.shape[2], kv_w)

    x = jnp.concatenate([x_prompt.reshape(n_ctx, d), x_sample.reshape(n_lat, d)], axis=0)
    gate_spec = lambda layer, which: pl.BlockSpec(
        (None, None, None, 1, tn), lambda i, j, *_: (layer, rows.mod_row(i, tm), which, 0, j))
    w_tile = lambda layer, k_rows, k_blk: pl.BlockSpec((None, k_rows, tn), lambda i, j: (layer, k_blk, j))
    b1 = b_mlp1.reshape(depth, 1, d_ff)
    b2 = b_mlp2.reshape(depth, 1, d)

    new_k, new_v = [], []
    for l in range(depth):
        n1 = _norm_mod(x, norm1_g, mod, l, scale_a, shift_a, rows)
        u = _mm([n1], [(d, 0)], [w_in_main], [w_tile(l, d, 0)], main_cols, BF16, tm=tm, tn=tn,
                name="in_proj_main")
        ukv = _mm([n1], [(d, 0)], [w_in_kv], [w_tile(l, d, 0)], 2 * kv_w, F32, tm=tm, tn=tn,
                  name="in_proj_kv")
        new_k.append(ukv[:n_ctx, :kv_w].reshape(batch, seq, n_kv, HEAD_DIM))
        new_v.append(ukv[:n_ctx, kv_w:].reshape(batch, seq, n_kv, HEAD_DIM))

        xcs = _fourier_chan(u, blk_fourier, chan_cs, f_w, tm)
        yf = jnp.concatenate([
            _fourier_pos(xcs, dft_ctx, wf_b, l, 0, batch, seq).reshape(n_ctx, f_w),
            _fourier_pos(xcs, dft_lat, wf_b, l, n_ctx, dec_batch, dec_seq).reshape(n_lat, f_w)], axis=0)

        yc = _conv(u, blk_glu, w_dw, b_dw, conv_ln_g, conv_ln_b, w_pw_b, l, rows, seq)

        att = jnp.concatenate([
            _attn_ctx(u, 0, ukv, sink, l, batch, seq, n_kv),
            _attn_lat(u, 0, ukv, cs_lat, cache_k2, cache_v2, sink, l, n_ctx, dec_batch, dec_seq, n_kv)],
            axis=0)

        x = _mm([yf, yc, att], [(f_w, 0), (conv_ch, 0), (q_w, 0)], [w_out_b, w_out_b, w_out_b],
                [w_tile(l, f_w, 0), w_tile(l, conv_ch, 1), w_tile(l, q_w, 1)],
                d, F32, tm=tm, tn=tn, res=(x, mod, gate_spec(l, gate_a)), name="out_proj")

        n2 = _norm_mod(x, norm2_g, mod, l, scale_m, shift_m, rows)
        h = _mm([n2], [(d, 0)], [w1_b], [w_tile(l, d, 0)], d_ff, BF16, tm=tm, tn=tn,
                bias=(b1, pl.BlockSpec((None, 1, tn), lambda i, j, l=l: (l, 0, j))),
                relu2=True, name="mlp_up")
        x = _mm_kgrid_res(h, w2_b, l, b2, x, mod, gate_spec(l, gate_m),
                          tm=tm, tn=tn, tk=min(2048, d_ff), name="mlp_down")

    y_prompt = _final_norm(x, final_g, 0, n_ctx).reshape(batch, seq, d)
    y_sample = _final_norm(x, final_g, n_ctx, n_lat).reshape(dec_batch, dec_seq, d)
    return (y_prompt, y_sample, jnp.stack(new_k, axis=1), jnp.stack(new_v, axis=1))
```

```python
import functools
import math

import jax
import jax.numpy as jnp
from jax import lax
from jax.experimental import pallas as pl
from jax.experimental.pallas import tpu as pltpu

GRID_W = 64
HEAD_DIM = 128
GROUP = 4
F_GROUPS = 8
CONV_K = 31
CONV_HALO = 16
SUBLANES = 8
DFT_EXTRA_ROWS = 16
WINDOW = 128
BLK = 128
ROPE_BASE = 10000.0
EPS = 1e-6
NEG_INF = -1e30
SCALE = HEAD_DIM ** -0.5
LOG2_E = math.log2(math.e)

V7X_VMEM_BYTES = 64 * 1024 * 1024
VMEM_LIMIT_BYTES = V7X_VMEM_BYTES - 8 * 1024 * 1024

F32 = jnp.float32
BF16 = jnp.bfloat16


def _params(*semantics):
    return pltpu.CompilerParams(dimension_semantics=semantics, vmem_limit_bytes=VMEM_LIMIT_BYTES)


def _pick(n, candidates):
    for c in candidates:
        if n % c == 0:
            return c
    raise ValueError(f"no tile in {candidates} divides {n}")


class _Rows:
    def __init__(self, n_ctx, t_lat):
        self.n_ctx, self.t_lat = n_ctx, t_lat

    def mod_row(self, i, tm):
        n_ctx_tiles = self.n_ctx // tm
        per_seq = self.t_lat // tm
        return jnp.where(i < n_ctx_tiles, 0, 1 + jnp.maximum(i - n_ctx_tiles, 0) // per_seq)


def _split_bf16(v):
    hi = v.astype(BF16)
    return hi, (v - hi.astype(F32)).astype(BF16)


def _ada_kernel(c_ref, w_ref, b_ref, o_ref, *, k_chunk):
    acc = b_ref[...] + jnp.zeros(o_ref.shape, F32)
    for k0 in range(0, c_ref.shape[1], k_chunk):
        c = c_ref[:, k0:k0 + k_chunk]
        s_hi, s_lo = _split_bf16(c * jax.nn.sigmoid(c))
        w_hi, w_lo = _split_bf16(w_ref[k0:k0 + k_chunk, :])
        acc = acc + jnp.dot(s_hi, w_hi, preferred_element_type=F32)
        acc = acc + jnp.dot(s_lo, w_hi, preferred_element_type=F32)
        acc = acc + jnp.dot(s_hi, w_lo, preferred_element_type=F32)
    o_ref[...] = acc


def _ada(cond, w_ada, b_ada):
    depth, d, n = w_ada.shape
    rows = cond.shape[0]
    tn = _pick(n, (512, 256, 128))
    return pl.pallas_call(
        functools.partial(_ada_kernel, k_chunk=_pick(d, (512, 256, 128))),
        grid=(depth, n // tn),
        in_specs=[
            pl.BlockSpec((rows, d), lambda l, j: (0, 0)),
            pl.BlockSpec((None, d, tn), lambda l, j: (l, 0, j)),
            pl.BlockSpec((None, 1, tn), lambda l, j: (l, 0, j)),
        ],
        out_specs=pl.BlockSpec((None, rows, tn), lambda l, j: (l, 0, j)),
        out_shape=jax.ShapeDtypeStruct((depth, rows, n), F32),
        compiler_params=_params("parallel", "parallel"),
        name="ada_modulation",
    )(cond, w_ada, b_ada.reshape(depth, 1, n))


def _norm_mod_kernel(x_ref, g_ref, scale_ref, shift_ref, o_ref):
    x = x_ref[...]
    y = x * lax.rsqrt(jnp.mean(x * x, axis=-1, keepdims=True) + EPS) * g_ref[...]
    o_ref[...] = (y * (1.0 + scale_ref[...]) + shift_ref[...]).astype(o_ref.dtype)


def _norm_mod(x, g, mod, layer, which_scale, which_shift, rows):
    m, d = x.shape
    tm = _pick(math.gcd(rows.n_ctx, rows.t_lat), (512, 256, 128))
    mod_spec = lambda which: pl.BlockSpec(
        (None, None, None, 1, d), lambda i: (layer, rows.mod_row(i, tm), which, 0, 0))
    return pl.pallas_call(
        _norm_mod_kernel,
        grid=(m // tm,),
        in_specs=[
            pl.BlockSpec((tm, d), lambda i: (i, 0)),
            pl.BlockSpec((None, 1, d), lambda i: (layer, 0, 0)),
            mod_spec(which_scale),
            mod_spec(which_shift),
        ],
        out_specs=pl.BlockSpec((tm, d), lambda i: (i, 0)),
        out_shape=jax.ShapeDtypeStruct((m, d), BF16),
        compiler_params=_params("parallel"),
        name="rmsnorm_modulate",
    )(x, g.reshape(g.shape[0], 1, d), mod, mod)


def _final_norm_kernel(x_ref, g_ref, o_ref):
    x = x_ref[...]
    y = x * lax.rsqrt(jnp.mean(x * x, axis=-1, keepdims=True) + EPS)
    o_ref[...] = y * g_ref[...]


def _final_norm(x, g, row0, n_rows):
    d = x.shape[1]
    tm = _pick(math.gcd(row0, n_rows) if row0 else n_rows, (512, 256, 128))
    off = row0 // tm
    return pl.pallas_call(
        _final_norm_kernel,
        grid=(n_rows // tm,),
        in_specs=[
            pl.BlockSpec((tm, d), lambda i: (off + i, 0)),
            pl.BlockSpec((1, d), lambda i: (0, 0)),
        ],
        out_specs=pl.BlockSpec((tm, d), lambda i: (i, 0)),
        out_shape=jax.ShapeDtypeStruct((n_rows, d), F32),
        compiler_params=_params("parallel"),
        name="final_rmsnorm",
    )(x, g.reshape(1, d))


def _mm_kernel(*refs, n_pairs, has_bias, relu2, has_res):
    it = iter(refs)
    lhs = [next(it) for _ in range(n_pairs)]
    rhs = [next(it) for _ in range(n_pairs)]
    bias_ref = next(it) if has_bias else None
    x_ref = next(it) if has_res else None
    gate_ref = next(it) if has_res else None
    o_ref = next(it)
    acc = None
    for a_ref, b_ref in zip(lhs, rhs):
        part = jnp.dot(a_ref[...], b_ref[...], preferred_element_type=F32)
        acc = part if acc is None else acc + part
    if has_bias:
        acc = acc + bias_ref[...]
    if relu2:
        acc = jnp.square(jnp.maximum(acc, 0.0))
    if has_res:
        acc = x_ref[...] + gate_ref[...] * acc
    o_ref[...] = acc.astype(o_ref.dtype)


def _mm(lhs, lhs_cols, rhs, rhs_specs, n_out, out_dtype, *, tm, tn, bias=None, relu2=False, res=None, name):
    m = lhs[0].shape[0]
    in_specs = [pl.BlockSpec((tm, w), lambda i, j, blk=blk: (i, blk)) for w, blk in lhs_cols]
    in_specs += list(rhs_specs)
    args = list(lhs) + list(rhs)
    if bias is not None:
        args.append(bias[0])
        in_specs.append(bias[1])
    if res is not None:
        x, gate, gate_spec = res
        args += [x, gate]
        in_specs += [pl.BlockSpec((tm, tn), lambda i, j: (i, j)), gate_spec]
    return pl.pallas_call(
        functools.partial(_mm_kernel, n_pairs=len(lhs), has_bias=bias is not None, relu2=relu2,
                          has_res=res is not None),
        grid=(m // tm, n_out // tn),
        in_specs=in_specs,
        out_specs=pl.BlockSpec((tm, tn), lambda i, j: (i, j)),
        out_shape=jax.ShapeDtypeStruct((m, n_out), out_dtype),
        compiler_params=_params("parallel", "parallel"),
        name=name,
    )(*args)


def _mm_kgrid_kernel(a_ref, b_ref, bias_ref, x_ref, gate_ref, o_ref, acc_ref):
    k = pl.program_id(2)

    @pl.when(k == 0)
    def _():
        acc_ref[...] = jnp.zeros_like(acc_ref)

    acc_ref[...] += jnp.dot(a_ref[...], b_ref[...], preferred_element_type=F32)

    @pl.when(k == pl.num_programs(2) - 1)
    def _():
        o_ref[...] = x_ref[...] + gate_ref[...] * (acc_ref[...] + bias_ref[...])


def _mm_kgrid_res(a, w, layer, bias, x, gate, gate_spec, *, tm, tn, tk, name):
    m, k = a.shape
    n = w.shape[2]
    return pl.pallas_call(
        _mm_kgrid_kernel,
        grid=(m // tm, n // tn, k // tk),
        in_specs=[
            pl.BlockSpec((tm, tk), lambda i, j, kk: (i, kk)),
            pl.BlockSpec((None, tk, tn), lambda i, j, kk: (layer, kk, j)),
            pl.BlockSpec((None, 1, tn), lambda i, j, kk: (layer, 0, j)),
            pl.BlockSpec((tm, tn), lambda i, j, kk: (i, j)),
            gate_spec,
        ],
        out_specs=pl.BlockSpec((tm, tn), lambda i, j, kk: (i, j)),
        out_shape=jax.ShapeDtypeStruct((m, n), F32),
        scratch_shapes=[pltpu.VMEM((tm, tn), F32)],
        compiler_params=_params("parallel", "parallel", "arbitrary"),
        name=name,
    )(a, w, bias, x, gate)


def _fourier_chan_kernel(x_ref, cs_ref, o_ref):
    f_gc = cs_ref.shape[0]
    f_w = x_ref.shape[1]
    for g in range(f_w // f_gc):
        cols = slice(g * f_gc, (g + 1) * f_gc)
        r = jnp.dot(x_ref[:, cols], cs_ref[...], preferred_element_type=F32)
        o_ref[:, cols] = r[:, :f_gc].astype(o_ref.dtype)
        o_ref[:, f_w + g * f_gc:f_w + (g + 1) * f_gc] = r[:, f_gc:].astype(o_ref.dtype)


def _fourier_chan(u, col_blk, chan_cs, f_w, tm):
    m = u.shape[0]
    return pl.pallas_call(
        _fourier_chan_kernel,
        grid=(m // tm,),
        in_specs=[
            pl.BlockSpec((tm, f_w), lambda i: (i, col_blk)),
            pl.BlockSpec(chan_cs.shape, lambda i: (0, 0)),
        ],
        out_specs=pl.BlockSpec((tm, 2 * f_w), lambda i: (i, 0)),
        out_shape=jax.ShapeDtypeStruct((m, 2 * f_w), BF16),
        compiler_params=_params("parallel"),
        name="fourier_channel",
    )(u, chan_cs)


def _fourier_pos_kernel(ct_ref, ctx_ref, st_ref, stx_ref, xc_ref, xs_ref, wf_ref, jrev_ref, lo_ref, hi_ref, d_ref):
    tq = ct_ref.shape[0]
    c_rows = jnp.concatenate([ct_ref[...], ctx_ref[...]], axis=0)
    s_rows = jnp.concatenate([st_ref[...], stx_ref[...]], axis=0)
    p = jnp.dot(c_rows, xc_ref[...], preferred_element_type=F32)
    q_neg = jnp.dot(s_rows, xs_ref[...], preferred_element_type=F32)
    d_ref[...] = p - q_neg
    lo = (p[:tq] + q_neg[:tq]).astype(BF16)
    up = d_ref[1:tq + 1, :].astype(BF16)
    f_gc = wf_ref.shape[1]
    z_up = []
    for g in range(wf_ref.shape[0]):
        cols = slice(g * f_gc, (g + 1) * f_gc)
        lo_ref[:, cols] = jnp.dot(lo[:, cols], wf_ref[g], preferred_element_type=F32).astype(lo_ref.dtype)
        z_up.append(jnp.dot(up[:, cols], wf_ref[g], preferred_element_type=F32).astype(BF16))
    hi_ref[...] = jnp.dot(jrev_ref[...], jnp.concatenate(z_up, axis=1),
                          preferred_element_type=F32).astype(hi_ref.dtype)


def _fourier_pos(xcs, dft, wf, layer, row0, n_seq, t):
    ct, st_neg, jrev = dft
    f_w = xcs.shape[1] // 2
    tq = jrev.shape[0]
    nj = t // (2 * tq)
    seq0 = row0 // t
    xb = DFT_EXTRA_ROWS
    main = lambda: pl.BlockSpec((tq, t), lambda b, j: (j, 0))
    extra = lambda: pl.BlockSpec((xb, t), lambda b, j: ((j + 1) * (tq // xb), 0))
    half = jax.ShapeDtypeStruct((n_seq * (t // 2), f_w), BF16)
    lo, hi = pl.pallas_call(
        _fourier_pos_kernel,
        grid=(n_seq, nj),
        in_specs=[
            main(), extra(), main(), extra(),
            pl.BlockSpec((t, f_w), lambda b, j: (seq0 + b, 0)),
            pl.BlockSpec((t, f_w), lambda b, j: (seq0 + b, 1)),
            pl.BlockSpec((None,) + wf.shape[1:], lambda b, j: (layer, 0, 0, 0)),
            pl.BlockSpec((tq, tq), lambda b, j: (0, 0)),
        ],
        out_specs=[pl.BlockSpec((tq, f_w), lambda b, j: (b * nj + j, 0)),
                   pl.BlockSpec((tq, f_w), lambda b, j: (b * nj + nj - 1 - j, 0))],
        out_shape=[half, half],
        scratch_shapes=[pltpu.VMEM((tq + xb, f_w), F32)],
        compiler_params=_params("parallel", "parallel"),
        name=f"fourier_position_t{t}",
    )(ct, ct, st_neg, st_neg, xcs, xcs, wf, jrev)
    return jnp.stack([lo.reshape(n_seq, t // 2, f_w), hi.reshape(n_seq, t // 2, f_w)], axis=1)


def _dft_tables(t):
    tq = _pick(t // 2, (256, 128))
    rows = jnp.arange(t // 2 + DFT_EXTRA_ROWS, dtype=jnp.int32)
    cols = jnp.arange(t, dtype=jnp.int32)
    ang = ((rows[:, None] * cols[None, :]) % t).astype(F32) * (2.0 * math.pi / t)
    s = t ** -0.5
    jrev = jnp.eye(tq, dtype=BF16)[::-1]
    return (jnp.cos(ang) * s).astype(BF16), (-jnp.sin(ang) * s).astype(BF16), jrev


def _conv_kernel(a_ref, g_ref, ap_ref, gp_ref, an_ref, gn_ref, wdw_ref, bdw_ref, lng_ref, lnb_ref,
                 wpw_ref, o_ref, hext_ref, shift_ref, conv_ref, *, rows, tt, row_chunk, lane_chunk):
    i = pl.program_id(0)
    n_ctx_tiles = rows.n_ctx // tt
    per_seq = rows.t_lat // tt
    pos = jnp.where(i < n_ctx_tiles, 0, jnp.maximum(i - n_ctx_tiles, 0) % per_seq)
    last = jnp.where(i < n_ctx_tiles, 0, per_seq - 1)

    def glu(a, g):
        return a[...].astype(F32) * jax.nn.sigmoid(g[...].astype(F32))

    halo = CONV_HALO
    hext_ref[0:halo, :] = jnp.where(pos > 0, glu(ap_ref, gp_ref), 0.0)
    hext_ref[halo:halo + tt, :] = glu(a_ref, g_ref)
    hext_ref[halo + tt:2 * halo + tt, :] = jnp.where(pos < last, glu(an_ref, gn_ref), 0.0)

    n_shift_rows = shift_ref.shape[1]
    for s in range(SUBLANES):
        shift_ref[s] = hext_ref[s:s + n_shift_rows, :]

    c = conv_ref.shape[1]
    first_tap = halo - CONV_K // 2

    groups = row_chunk // SUBLANES

    def chunk(r, carry):
        r0 = pl.multiple_of(r * row_chunk, row_chunk)
        for l0 in range(0, c, lane_chunk):
            lanes = slice(l0, l0 + lane_chunk)
            accs = [bdw_ref[:, lanes]] * groups
            for s in range(SUBLANES):
                qs = [q for q in range(-(-CONV_K // SUBLANES) + 1) if 0 <= q * SUBLANES + s - first_tap < CONV_K]
                taps = {q: wdw_ref[q * SUBLANES + s - first_tap, :, lanes] for q in qs}
                for u in range(min(qs), max(qs) + groups):
                    start = pl.multiple_of(r0 + u * SUBLANES, SUBLANES)
                    h = shift_ref[s, pl.ds(start, SUBLANES), lanes]
                    for q in qs:
                        if 0 <= u - q < groups:
                            accs[u - q] = accs[u - q] + taps[q] * h
            for a, acc in enumerate(accs):
                conv_ref[pl.ds(pl.multiple_of(r0 + a * SUBLANES, SUBLANES), SUBLANES), lanes] = acc
        return carry

    lax.fori_loop(0, tt // row_chunk, chunk, 0)

    h = conv_ref[...]
    mu = jnp.mean(h, axis=-1, keepdims=True)
    var = jnp.mean(jnp.square(h - mu), axis=-1, keepdims=True)
    y = (h - mu) * lax.rsqrt(var + EPS) * lng_ref[...] + lnb_ref[...]
    y = y * jax.nn.sigmoid(y)
    o_ref[...] = jnp.dot(y.astype(BF16), wpw_ref[...], preferred_element_type=F32).astype(o_ref.dtype)


def _conv(u, col_a, w_dw, b_dw, ln_g, ln_b, w_pw, layer, rows, seq_ctx):
    m = u.shape[0]
    c = w_pw.shape[1]
    tt = seq_ctx
    assert rows.t_lat % tt == 0 and rows.n_ctx % tt == 0 and tt % CONV_HALO == 0
    hb = tt // CONV_HALO
    n_hblocks = m // CONV_HALO
    cur = lambda col: pl.BlockSpec((tt, c), lambda i: (i, col))
    prev = lambda col: pl.BlockSpec((CONV_HALO, c), lambda i: (jnp.maximum(i * hb - 1, 0), col))
    nxt = lambda col: pl.BlockSpec((CONV_HALO, c), lambda i: (jnp.minimum((i + 1) * hb, n_hblocks - 1), col))
    vec = lambda: pl.BlockSpec((None, 1, c), lambda i: (layer, 0, 0))
    depth, taps = w_dw.shape[:2]
    w_dw_rep = jnp.broadcast_to(w_dw[:, :, None, :], (depth, taps, SUBLANES, c))
    b_dw_rep = jnp.broadcast_to(b_dw[:, None, :], (depth, SUBLANES, c))
    return pl.pallas_call(
        functools.partial(_conv_kernel, rows=rows, tt=tt, row_chunk=32, lane_chunk=512),
        grid=(m // tt,),
        in_specs=[
            cur(col_a), cur(col_a + 1), prev(col_a), prev(col_a + 1), nxt(col_a), nxt(col_a + 1),
            pl.BlockSpec((None, taps, SUBLANES, c), lambda i: (layer, 0, 0, 0)),
            pl.BlockSpec((None, SUBLANES, c), lambda i: (layer, 0, 0)),
            vec(), vec(),
            pl.BlockSpec((None, c, c), lambda i: (layer, 0, 0)),
        ],
        out_specs=pl.BlockSpec((tt, c), lambda i: (i, 0)),
        out_shape=jax.ShapeDtypeStruct((m, c), BF16),
        scratch_shapes=[pltpu.VMEM((tt + 2 * CONV_HALO, c), F32),
                        pltpu.VMEM((SUBLANES, tt + 2 * CONV_HALO - SUBLANES, c), F32),
                        pltpu.VMEM((tt, c), F32)],
        compiler_params=_params("parallel"),
        name="conformer_conv",
    )(u, u, u, u, u, u, w_dw_rep, b_dw_rep, ln_g.reshape(-1, 1, c), ln_b.reshape(-1, 1, c), w_pw)


def _attend_t(qs, keys, vals, masks, sink_row):
    nt = (((1,), (1,)), ((), ()))
    tn = (((0,), (0,)), ((), ()))
    sink_row = sink_row * LOG2_E
    scores = []
    for k, mask in zip(keys, masks):
        s = lax.dot_general(k, qs, nt, preferred_element_type=F32) * (SCALE * LOG2_E)
        scores.append(s if mask is None else mask(s))
    m = sink_row
    for s in scores:
        m = jnp.maximum(m, jnp.max(s, axis=0, keepdims=True))
    den = jnp.exp2(sink_row - m)
    out = None
    for s, v in zip(scores, vals):
        p = jnp.exp2(s - m)
        den = den + jnp.sum(p, axis=0, keepdims=True)
        pv = lax.dot_general(v, p.astype(BF16), tn, preferred_element_type=F32)
        out = pv if out is None else out + pv
    return out / den


def _sink_row(sink_ref, layer, head0, queries_per_head):
    return jnp.concatenate(
        [jnp.full((1, queries_per_head), sink_ref[layer, head0 + g], F32) for g in range(GROUP)], axis=1)


def _attn_ctx_kernel(sink_ref, q_ref, k_ref, v_ref, o_ref, *, layer):
    kv = pl.program_id(1)
    t = q_ref.shape[0]
    qs = jnp.concatenate([q_ref[:, g * HEAD_DIM:(g + 1) * HEAD_DIM] for g in range(GROUP)], axis=0)
    out_t = _attend_t(qs, [k_ref[...].astype(BF16)], [v_ref[...].astype(BF16)], [None],
                      _sink_row(sink_ref, layer, kv * GROUP, t))
    for g in range(GROUP):
        o_ref[:, g * HEAD_DIM:(g + 1) * HEAD_DIM] = out_t[:, g * t:(g + 1) * t].T.astype(o_ref.dtype)


def _attn_ctx(u, q_col0, ukv, sink, layer, n_seq, t, n_kv):
    qw = GROUP * HEAD_DIM
    return pl.pallas_call(
        functools.partial(_attn_ctx_kernel, layer=layer),
        grid=(n_seq, n_kv),
        in_specs=[
            pl.BlockSpec(memory_space=pltpu.SMEM),
            pl.BlockSpec((t, qw), lambda b, kv: (b, q_col0 + kv)),
            pl.BlockSpec((t, HEAD_DIM), lambda b, kv: (b, kv)),
            pl.BlockSpec((t, HEAD_DIM), lambda b, kv: (b, n_kv + kv)),
        ],
        out_specs=pl.BlockSpec((t, qw), lambda b, kv: (b, kv)),
        out_shape=jax.ShapeDtypeStruct((n_seq * t, n_kv * qw), BF16),
        compiler_params=_params("parallel", "parallel"),
        name="attention_context",
    )(sink, u, ukv, ukv)


def _rope(x, cs):
    cos, sin = cs[:, :HEAD_DIM], cs[:, HEAD_DIM:]
    lane = lax.broadcasted_iota(jnp.int32, x.shape, 1)
    quarter = HEAD_DIM // 4
    partner = jnp.where((lane & quarter) == 0,
                        pltpu.roll(x, HEAD_DIM - quarter, axis=1), pltpu.roll(x, quarter, axis=1))
    return x * cos + partner * sin


def _attn_lat_kernel(sink_ref, q_ref, kvp_ref, kvc_ref, kvn_ref, csp_ref, csc_ref, csn_ref, ck_ref, cv_ref,
                     o_ref, *, layer, n_kv):
    i = pl.program_id(1)
    nb = pl.num_programs(1)
    kw = n_kv * HEAD_DIM
    rq = GROUP * BLK
    j = lax.broadcasted_iota(jnp.int32, (BLK, rq), 0)
    r = lax.broadcasted_iota(jnp.int32, (BLK, rq), 1) % BLK
    see_prev = (j >= r) & (i > 0)
    see_next = (j <= r) & (i < nb - 1)

    def window_mask(s):
        return jnp.concatenate([jnp.where(see_prev, s[:BLK], NEG_INF), s[BLK:2 * BLK],
                                jnp.where(see_next, s[2 * BLK:], NEG_INF)], axis=0)

    cs_q = csc_ref[...]
    for kv in range(n_kv):
        hs = slice(kv * HEAD_DIM, (kv + 1) * HEAD_DIM)
        vs = slice(kw + kv * HEAD_DIM, kw + (kv + 1) * HEAD_DIM)
        keys = jnp.concatenate(
            [_rope(ref[:, hs], cs[...]).astype(BF16)
             for ref, cs in ((kvp_ref, csp_ref), (kvc_ref, csc_ref), (kvn_ref, csn_ref))], axis=0)
        vals = jnp.concatenate([ref[:, vs].astype(BF16) for ref in (kvp_ref, kvc_ref, kvn_ref)], axis=0)
        qs = jnp.concatenate(
            [_rope(q_ref[:, (kv * GROUP + g) * HEAD_DIM:(kv * GROUP + g + 1) * HEAD_DIM].astype(F32),
                   cs_q).astype(BF16) for g in range(GROUP)], axis=0)
        out_t = _attend_t(qs, [ck_ref[:, hs].astype(BF16), keys], [cv_ref[:, hs].astype(BF16), vals],
                          [None, window_mask], _sink_row(sink_ref, layer, kv * GROUP, BLK))
        for g in range(GROUP):
            h = kv * GROUP + g
            o_ref[:, h * HEAD_DIM:(h + 1) * HEAD_DIM] = out_t[:, g * BLK:(g + 1) * BLK].T.astype(o_ref.dtype)


def _attn_lat(u, q_col0, ukv, cs, cache_k, cache_v, sink, layer, row0, n_seq, t, n_kv):
    qw = n_kv * GROUP * HEAD_DIM
    kw = n_kv * HEAD_DIM
    nb = t // BLK
    blk0 = row0 // BLK
    past = cache_k.shape[2]
    kv_spec = lambda f: pl.BlockSpec((BLK, 2 * kw), lambda b, i: (blk0 + b * nb + f(i), 0))
    cs_spec = lambda f: pl.BlockSpec((BLK, 2 * HEAD_DIM), lambda b, i: (f(i), 0))
    prev = lambda i: jnp.maximum(i - 1, 0)
    cur = lambda i: i
    nxt = lambda i: jnp.minimum(i + 1, nb - 1)
    cache_spec = pl.BlockSpec((None, None, past, kw), lambda b, i: (b, layer, 0, 0))
    return pl.pallas_call(
        functools.partial(_attn_lat_kernel, layer=layer, n_kv=n_kv),
        grid=(n_seq, nb),
        in_specs=[
            pl.BlockSpec(memory_space=pltpu.SMEM),
            pl.BlockSpec((BLK, qw), lambda b, i: (blk0 + b * nb + i, q_col0)),
            kv_spec(prev), kv_spec(cur), kv_spec(nxt),
            cs_spec(prev), cs_spec(cur), cs_spec(nxt),
            cache_spec, cache_spec,
        ],
        out_specs=pl.BlockSpec((BLK, qw), lambda b, i: (b * nb + i, 0)),
        out_shape=jax.ShapeDtypeStruct((n_seq * t, qw), BF16),
        compiler_params=_params("parallel", "parallel"),
        name="attention_latent",
    )(sink, u, ukv, ukv, ukv, cs, cs, cs, cache_k, cache_v)


def _rope_table(t):
    rot_axis = HEAD_DIM // 2
    pos = jnp.arange(t)
    row = (pos // GRID_W).astype(F32)
    col = (pos % GRID_W).astype(F32)
    inv = ROPE_BASE ** (-jnp.arange(0, rot_axis, 2, dtype=F32) / rot_axis)
    ar, ac = row[:, None] * inv, col[:, None] * inv
    cos = jnp.concatenate([jnp.cos(ar), jnp.cos(ar), jnp.cos(ac), jnp.cos(ac)], axis=-1)
    sin = jnp.concatenate([-jnp.sin(ar), jnp.sin(ar), -jnp.sin(ac), jnp.sin(ac)], axis=-1)
    return jnp.concatenate([cos, sin], axis=-1)


def kernel(x_prompt, x_sample, cache_k, cache_v, c, c_ctx, norm1_g, norm2_g, w_ada, b_ada, w_in, w_fourier, w_dw, b_dw, conv_ln_g, conv_ln_b, w_pw, sink, w_out, w_mlp1, b_mlp1, w_mlp2, b_mlp2, final_g):
    batch, seq, d = x_prompt.shape
    dec_batch, dec_seq, _ = x_sample.shape
    depth = w_in.shape[0]
    n_heads = sink.shape[1]
    n_kv = n_heads // GROUP
    q_w, kv_w = n_heads * HEAD_DIM, n_kv * HEAD_DIM
    f_w = w_fourier.shape[1] * w_fourier.shape[2]
    conv_ch = w_pw.shape[1]
    in_cols = w_in.shape[2]
    d_ff = w_mlp1.shape[2]
    assert in_cols == f_w + 2 * conv_ch + q_w + 2 * kv_w and w_fourier.shape[1] == F_GROUPS
    assert f_w == conv_ch == GROUP * HEAD_DIM * 2 and q_w == 2 * f_w, "column blocks below assume these widths"
    assert w_dw.shape[1] == CONV_K and dec_seq % GRID_W == 0 and WINDOW == BLK

    n_ctx, n_lat = batch * seq, dec_batch * dec_seq
    m = n_ctx + n_lat
    tn = 1024
    main_cols = q_w + f_w + 2 * conv_ch
    q_split = f_w + 2 * conv_ch
    blk_fourier, blk_glu = q_w // f_w, q_w // f_w + 1

    cond_rows = -(-(1 + dec_batch) // 8) * 8
    cond = jnp.zeros((cond_rows, d), F32).at[0].set(c_ctx).at[1:1 + dec_batch].set(c)
    mod = _ada(cond, w_ada, b_ada).reshape(depth, cond_rows, 6, 1, d)
    shift_a, scale_a, gate_a, shift_m, scale_m, gate_m = range(6)

    w_in_main = jnp.concatenate(
        [w_in[:, :, q_split:q_split + q_w].astype(BF16), w_in[:, :, :q_split].astype(BF16)], axis=2)
    w_in_kv = w_in[:, :, q_split + q_w:].astype(BF16)
    w_out_b = w_out.astype(BF16)
    w1_b, w2_b, w_pw_b = w_mlp1.astype(BF16), w_mlp2.astype(BF16), w_pw.astype(BF16)
    f_gc = f_w // F_GROUPS
    wf_b = w_fourier.astype(BF16)
    gi = jnp.arange(f_gc, dtype=jnp.int32)
    ang_ch = ((gi[:, None] * gi[None, :]) % f_gc).astype(F32) * (2.0 * math.pi / f_gc)
    chan_cs = (jnp.concatenate([jnp.cos(ang_ch), jnp.sin(ang_ch)], axis=1) * f_gc ** -0.5).astype(BF16)
    dft_ctx = _dft_tables(seq)
    dft_lat = _dft_tables(dec_seq)
    cs_lat = _rope_table(dec_seq)
    cache_k2 = cache_k.reshape(dec_batch, depth, cache_k.shape[2], kv_w)
    cache_v2 = cache_v.reshape(dec_batch, depth, cache_v.shape[2], kv_w)

    w_tile = lambda layer, k_rows, k_blk: pl.BlockSpec((None, k_rows, tn), lambda i, j: (layer, k_blk, j))
    b1 = b_mlp1.reshape(depth, 1, d_ff)
    b2 = b_mlp2.reshape(depth, 1, d)

    def run_stream(x, rows, is_ctx):
        tm = _pick(math.gcd(rows.n_ctx, rows.t_lat), (1024, 512, 256))
        gate_spec = lambda layer, which: pl.BlockSpec(
            (None, None, None, 1, tn), lambda i, j, *_: (layer, rows.mod_row(i, tm), which, 0, j))
        keys, vals = [], []
        for l in range(depth):
            n1 = _norm_mod(x, norm1_g, mod, l, scale_a, shift_a, rows)
            u = _mm([n1], [(d, 0)], [w_in_main], [w_tile(l, d, 0)], main_cols, BF16, tm=tm, tn=tn,
                    name="in_proj_main")
            ukv = _mm([n1], [(d, 0)], [w_in_kv], [w_tile(l, d, 0)], 2 * kv_w, F32, tm=tm, tn=tn,
                      name="in_proj_kv")
            if is_ctx:
                keys.append(ukv[:, :kv_w].reshape(batch, seq, n_kv, HEAD_DIM))
                vals.append(ukv[:, kv_w:].reshape(batch, seq, n_kv, HEAD_DIM))

            xcs = _fourier_chan(u, blk_fourier, chan_cs, f_w, tm)
            if is_ctx:
                yf = _fourier_pos(xcs, dft_ctx, wf_b, l, 0, batch, seq)
            else:
                yf = _fourier_pos(xcs, dft_lat, wf_b, l, 0, dec_batch, dec_seq)
            yf = yf.reshape(x.shape[0], f_w)

            yc = _conv(u, blk_glu, w_dw, b_dw, conv_ln_g, conv_ln_b, w_pw_b, l, rows, seq)

            if is_ctx:
                att = _attn_ctx(u, 0, ukv, sink, l, batch, seq, n_kv)
            else:
                att = _attn_lat(u, 0, ukv, cs_lat, cache_k2, cache_v2, sink, l, 0, dec_batch, dec_seq, n_kv)

            x = _mm([yf, yc, att], [(f_w, 0), (conv_ch, 0), (q_w, 0)], [w_out_b, w_out_b, w_out_b],
                    [w_tile(l, f_w, 0), w_tile(l, conv_ch, 1), w_tile(l, q_w, 1)],
                    d, F32, tm=tm, tn=tn, res=(x, mod, gate_spec(l, gate_a)), name="out_proj")

            n2 = _norm_mod(x, norm2_g, mod, l, scale_m, shift_m, rows)
            h = _mm([n2], [(d, 0)], [w1_b], [w_tile(l, d, 0)], d_ff, BF16, tm=tm, tn=tn,
                    bias=(b1, pl.BlockSpec((None, 1, tn), lambda i, j, l=l: (l, 0, j))),
                    relu2=True, name="mlp_up")
            x = _mm_kgrid_res(h, w2_b, l, b2, x, mod, gate_spec(l, gate_m),
                              tm=tm, tn=tn, tk=min(2048, d_ff), name="mlp_down")
        return _final_norm(x, final_g, 0, x.shape[0]), keys, vals

    y_ctx, new_k, new_v = run_stream(x_prompt.reshape(n_ctx, d), _Rows(n_ctx, n_ctx), True)
    y_lat, _, _ = run_stream(x_sample.reshape(n_lat, d), _Rows(0, dec_seq), False)
    return (y_ctx.reshape(batch, seq, d), y_lat.reshape(dec_batch, dec_seq, d),
            jnp.stack(new_k, axis=1), jnp.stack(new_v, axis=1))
```
